```python
import math
import jax
import jax.numpy as jnp
from jax import lax
import numpy as np

D_MODEL = 2048
BATCH = 32
SEQ = 256
DEPTH = 4
DEC_BATCH = 8
DEC_SEQ = 1024
PAST_LEN = 256

GRID_W = 64
BRANCH_W = D_MODEL // 2
CONV_K = 5
SSD_HEAD_DIM = 64
SSD_HEADS = BRANCH_W // SSD_HEAD_DIM
SSD_GROUPS = 2
SSD_STATE = 128
SSD_CHUNK = 64
SSD_XBC = BRANCH_W + 2 * SSD_GROUPS * SSD_STATE
S5_GROUP_CH = 16
S5_GROUPS = BRANCH_W // S5_GROUP_CH
S5_STATE = 64
DN_HEAD_DIM = 128
DN_HEADS = BRANCH_W // DN_HEAD_DIM
DN_CHUNK = 64
N_BRANCH = 3
N_EXPERTS = 32
TOP_K = 4
D_FF = D_MODEL
SWIGLU_ALPHA = 1.702
SWIGLU_LIMIT = 7.0
MOE_BLOCK = 256
NORM_EPS = 1e-6
IN_WIDTHS = (BRANCH_W, SSD_XBC, 2 * SSD_HEADS, BRANCH_W, 3 * BRANCH_W, BRANCH_W, 2 * DN_HEADS, 2 * DN_HEADS, N_BRANCH * D_MODEL)
IN_TOTAL = sum(IN_WIDTHS)

kernel_name = 'hybrid_prefix_flow_ssd_s5_gdn_moe_step'


def _rms_norm(x, w):
    x32 = x.astype(jnp.float32)
    y = x32 * lax.rsqrt(jnp.mean(x32 * x32, axis=-1, keepdims=True) + NORM_EPS)
    return (y * w.astype(jnp.float32)).astype(x.dtype)


def _l2norm(x):
    return x * lax.rsqrt(jnp.sum(x * x, axis=-1, keepdims=True) + 1e-6)


def _flip(t):
    return jnp.flip(t, axis=1)


def _dwconv(x, w):
    return lax.conv_general_dilated(
        x, w[:, None, :].astype(x.dtype), window_strides=(1,),
        padding=[(CONV_K // 2, CONV_K // 2)],
        dimension_numbers=('NWC', 'WIO', 'NWC'),
        feature_group_count=x.shape[-1])


def _grid_pos(n_tok, dtype):
    rows = n_tok // GRID_W
    r = jnp.repeat(jnp.arange(rows, dtype=jnp.float32), GRID_W)
    col = jnp.tile(jnp.arange(GRID_W, dtype=jnp.float32), rows)
    quarter = D_MODEL // 4
    omega = 1.0 / (10000.0 ** (jnp.arange(quarter, dtype=jnp.float32) / quarter))
    ang_r = r[:, None] * omega
    ang_c = col[:, None] * omega
    pos = jnp.concatenate([jnp.sin(ang_r), jnp.cos(ang_r), jnp.sin(ang_c), jnp.cos(ang_c)], axis=-1)
    return pos.astype(dtype)


def _segsum(a):
    n = a.shape[-1]
    cs = jnp.cumsum(a, axis=-1)
    diff = cs[..., :, None] - cs[..., None, :]
    mask = jnp.arange(n)[:, None] >= jnp.arange(n)[None, :]
    return jnp.where(mask, diff, -jnp.inf)


def _ssd_scan(x, dt, a_neg, bm, cm, h0):
    bsz, seqlen, nh, hd = x.shape
    ng, ns = bm.shape[2], bm.shape[3]
    nr = nh // ng
    nc = seqlen // SSD_CHUNK
    xd = (x * dt[..., None]).reshape(bsz, nc, SSD_CHUNK, ng, nr, hd)
    ad = (dt * a_neg).reshape(bsz, nc, SSD_CHUNK, ng, nr).transpose(0, 3, 4, 1, 2)
    bc = bm.reshape(bsz, nc, SSD_CHUNK, ng, ns)
    cc = cm.reshape(bsz, nc, SSD_CHUNK, ng, ns)
    a_cs = jnp.cumsum(ad, axis=-1)
    lmat = jnp.exp(_segsum(ad))
    cb = jnp.einsum('bclgn,bcsgn->bgcls', cc, bc)
    y_diag = jnp.einsum('bgrcls,bcsgrp->bclgrp', cb[:, :, None] * lmat, xd)
    decay_st = jnp.exp(a_cs[..., -1:] - a_cs).transpose(0, 3, 4, 1, 2)
    st = jnp.einsum('bclgn,bclgrp->bcgrpn', bc, xd * decay_st[..., None])
    st = jnp.concatenate([h0.reshape(bsz, 1, ng, nr, hd, ns), st], axis=1)
    chunk_decay = jnp.exp(_segsum(jnp.pad(a_cs[..., -1], ((0, 0), (0, 0), (0, 0), (1, 0)))))
    st_all = jnp.einsum('bgrzc,bcgrpn->bzgrpn', chunk_decay, st)
    in_decay = jnp.exp(a_cs).transpose(0, 3, 4, 1, 2)
    y_off = jnp.einsum('bclgn,bcgrpn->bclgrp', cc, st_all[:, :-1]) * in_decay[..., None]
    y = (y_diag + y_off).reshape(bsz, seqlen, nh, hd)
    return y, st_all[:, -1].reshape(bsz, nh, hd, ns)


def _s5_scan(u, lam_re, lam_im, log_step, b_re, b_im, h0_re, h0_im):
    step = jnp.exp(log_step)[:, None]
    ang = lam_im * step
    mag = jnp.exp(lam_re * step)
    abar_re = mag * jnp.cos(ang)
    abar_im = mag * jnp.sin(ang)
    num_re = abar_re - 1.0
    den = lam_re * lam_re + lam_im * lam_im
    coef_re = (num_re * lam_re + abar_im * lam_im) / den
    coef_im = (abar_im * lam_re - num_re * lam_im) / den
    bb_re = coef_re[..., None] * b_re - coef_im[..., None] * b_im
    bb_im = coef_re[..., None] * b_im + coef_im[..., None] * b_re
    bu_re = jnp.einsum('blgj,gpj->blgp', u, bb_re)
    bu_im = jnp.einsum('blgj,gpj->blgp', u, bb_im)
    bu_re = bu_re.at[:, 0].add(abar_re * h0_re - abar_im * h0_im)
    bu_im = bu_im.at[:, 0].add(abar_re * h0_im + abar_im * h0_re)
    a_re = jnp.broadcast_to(abar_re, bu_re.shape)
    a_im = jnp.broadcast_to(abar_im, bu_re.shape)

    def combine(e1, e2):
        a1r, a1i, b1r, b1i = e1
        a2r, a2i, b2r, b2i = e2
        return (a2r * a1r - a2i * a1i, a2r * a1i + a2i * a1r,
                a2r * b1r - a2i * b1i + b2r, a2r * b1i + a2i * b1r + b2i)

    _, _, h_re, h_im = lax.associative_scan(combine, (a_re, a_im, bu_re, bu_im), axis=1)
    return h_re, h_im


def _gated_delta(q, k, v, beta, logdec, s0):
    bsz, seqlen, nh, dk = q.shape
    dv = v.shape[-1]
    nc = seqlen // DN_CHUNK

    def chunk(t):
        return jnp.moveaxis(t.reshape((bsz, nc, DN_CHUNK) + t.shape[2:]), 3, 1)

    qc = chunk(q) * (dk ** -0.5)
    kc = chunk(k)
    vc = chunk(v)
    bc = chunk(beta)
    gc = jnp.cumsum(chunk(logdec), axis=-1)
    idx = jnp.arange(DN_CHUNK)
    incl = idx[:, None] >= idx[None, :]
    strict = idx[:, None] > idx[None, :]
    decay = jnp.exp(jnp.where(incl, gc[..., :, None] - gc[..., None, :], -jnp.inf))
    kb = kc * bc[..., None]
    a_strict = jnp.where(strict, jnp.einsum('bhnid,bhnjd->bhnij', kb, kc) * decay, 0.0)
    lhs = a_strict + jnp.eye(DN_CHUNK, dtype=a_strict.dtype)
    rhs = jnp.concatenate([vc * bc[..., None], kb * jnp.exp(gc)[..., None]], axis=-1)
    sol = lax.linalg.triangular_solve(lhs, rhs, left_side=True, lower=True, unit_diagonal=True)
    u_new, w = sol[..., :dv], sol[..., dv:]
    attn = jnp.einsum('bhnid,bhnjd->bhnij', qc, kc) * decay
    q_dec = qc * jnp.exp(gc)[..., None]
    k_dec = kc * jnp.exp(gc[..., -1:] - gc)[..., None]
    g_tot = jnp.exp(gc[..., -1])

    def step(s, xs):
        q_i, k_i, u_i, w_i, a_i, g_i = xs
        v_i = u_i - jnp.einsum('bhck,bhkv->bhcv', w_i, s)
        o_i = jnp.einsum('bhck,bhkv->bhcv', q_i, s) + jnp.einsum('bhcj,bhjv->bhcv', a_i, v_i)
        s = s * g_i[..., None, None] + jnp.einsum('bhck,bhcv->bhkv', k_i, v_i)
        return s, o_i

    xs = tuple(jnp.moveaxis(t, 2, 0) for t in (q_dec, k_dec, u_new, w, attn, g_tot))
    s_fin, o = lax.scan(step, s0, xs)
    o = jnp.moveaxis(jnp.moveaxis(o, 0, 2), 1, 3).reshape(bsz, seqlen, nh, dv)
    return o, s_fin


def _mixing_block(h, st_ssd, st_s5, st_dn, p):
    f32 = jnp.float32
    bsz, seqlen, _ = h.shape
    proj = h @ p['w_in']
    splits = np.cumsum(IN_WIDTHS)[:-1].tolist()
    z_a, xbc_a, dt_a, u_b, qkv_c, g_c, a_c, b_c, gate_raw = jnp.split(proj, splits, axis=-1)

    xbc = jax.nn.silu(_dwconv(xbc_a, p['ssd_conv_w']) + p['ssd_conv_b']).astype(f32)
    xa = xbc[..., :BRANCH_W].reshape(bsz, seqlen, SSD_HEADS, SSD_HEAD_DIM)
    ba = xbc[..., BRANCH_W:BRANCH_W + SSD_GROUPS * SSD_STATE].reshape(bsz, seqlen, SSD_GROUPS, SSD_STATE)
    ca = xbc[..., BRANCH_W + SSD_GROUPS * SSD_STATE:].reshape(bsz, seqlen, SSD_GROUPS, SSD_STATE)
    dt = jax.nn.softplus(dt_a.astype(f32).reshape(bsz, seqlen, 2, SSD_HEADS) + p['ssd_dt_bias'].astype(f32))
    a_neg = -jnp.exp(p['ssd_a_log'].astype(f32))
    s_a = st_ssd.astype(f32)
    y_f, sa_f = _ssd_scan(xa, dt[:, :, 0], a_neg[0], ba, ca, s_a[:, 0])
    y_bw, sa_b = _ssd_scan(_flip(xa), _flip(dt[:, :, 1]), a_neg[1], _flip(ba), _flip(ca), s_a[:, 1])
    y_a = y_f + _flip(y_bw) + p['ssd_d'].astype(f32)[:, None] * xa
    y_a = y_a.reshape(bsz, seqlen, BRANCH_W) * jax.nn.silu(z_a.astype(f32))
    y_a = _rms_norm(y_a, p['ssd_norm'])
    new_ssd = jnp.stack([sa_f, sa_b], axis=1)

    u = u_b.astype(f32).reshape(bsz, seqlen, S5_GROUPS, S5_GROUP_CH)
    s_b = st_s5.astype(f32)
    lam_re = p['s5_lam_re'].astype(f32)
    lam_im = p['s5_lam_im'].astype(f32)
    log_step = p['s5_log_step'].astype(f32)
    b_re = p['s5_b_re'].astype(f32)
    b_im = p['s5_b_im'].astype(f32)
    hf_re, hf_im = _s5_scan(u, lam_re[0], lam_im[0], log_step[0], b_re, b_im, s_b[:, 0, 0], s_b[:, 0, 1])
    hb_re, hb_im = _s5_scan(_flip(u), lam_re[1], lam_im[1], log_step[1], b_re, b_im, s_b[:, 1, 0], s_b[:, 1, 1])
    new_s5 = jnp.stack([jnp.stack([hf_re[:, -1], hf_im[:, -1]], axis=1),
                        jnp.stack([hb_re[:, -1], hb_im[:, -1]], axis=1)], axis=1)
    h_re = hf_re + _flip(hb_re)
    h_im = hf_im + _flip(hb_im)
    y_b = (jnp.einsum('blgp,gjp->blgj', h_re, p['s5_c_re'].astype(f32))
           - jnp.einsum('blgp,gjp->blgj', h_im, p['s5_c_im'].astype(f32)))
    y_b = y_b.reshape(bsz, seqlen, BRANCH_W) + p['s5_d'].astype(f32) * u.reshape(bsz, seqlen, BRANCH_W)
    y_b = jax.nn.gelu(y_b)
    glu = y_b @ p['s5_w_glu'].astype(f32)
    y_b = glu[..., :BRANCH_W] * jax.nn.sigmoid(glu[..., BRANCH_W:])

    qkv = jax.nn.silu(_dwconv(qkv_c, p['dn_conv_w'])).astype(f32)
    q, k, v = jnp.split(qkv, 3, axis=-1)
    q = _l2norm(q.reshape(bsz, seqlen, DN_HEADS, DN_HEAD_DIM))
    k = _l2norm(k.reshape(bsz, seqlen, DN_HEADS, DN_HEAD_DIM))
    v = v.reshape(bsz, seqlen, DN_HEADS, DN_HEAD_DIM)
    beta = jax.nn.sigmoid(b_c.astype(f32).reshape(bsz, seqlen, 2, DN_HEADS))
    logdec = -jnp.exp(p['dn_a_log'].astype(f32)) * jax.nn.softplus(
        a_c.astype(f32).reshape(bsz, seqlen, 2, DN_HEADS) + p['dn_dt_bias'].astype(f32))
    s_c = st_dn.astype(f32)
    o_f, sc_f = _gated_delta(q, k, v, beta[:, :, 0], logdec[:, :, 0], s_c[:, 0])
    o_bw, sc_b = _gated_delta(_flip(q), _flip(k), _flip(v), _flip(beta[:, :, 1]), _flip(logdec[:, :, 1]), s_c[:, 1])
    o = o_f + _flip(o_bw)
    o = _rms_norm(o, p['dn_norm']) * jax.nn.silu(g_c.astype(f32).reshape(bsz, seqlen, DN_HEADS, DN_HEAD_DIM))
    y_c = o.reshape(bsz, seqlen, BRANCH_W)
    new_dn = jnp.stack([sc_f, sc_b], axis=1)

    ys = jnp.stack([y_a, y_b, y_c], axis=2).astype(h.dtype)
    br = jnp.einsum('blkc,kcd->blkd', ys, p['w_branch'])
    gates = jax.nn.sigmoid(gate_raw.reshape(bsz, seqlen, N_BRANCH, D_MODEL))
    merged = jnp.sum(gates * br, axis=2)
    out = merged @ p['w_out']
    return out.astype(h.dtype), (new_ssd, new_s5, new_dn)


def _moe(h, w_router, b_router, w_up, b_up, w_down, b_down):
    f32 = jnp.float32
    bsz, seqlen, d = h.shape
    n_tok = bsz * seqlen
    t = h.reshape(n_tok, d)
    logits = t.astype(f32) @ w_router.astype(f32) + b_router.astype(f32)
    top_logit, top_e = lax.top_k(logits, TOP_K)
    top_w = jax.nn.softmax(top_logit, axis=-1)
    n_assign = n_tok * TOP_K
    flat_e = top_e.reshape(n_assign).astype(jnp.int32)
    order = jnp.argsort(flat_e)
    sorted_e = flat_e[order]
    counts = jax.ops.segment_sum(jnp.ones((n_assign,), jnp.int32), flat_e, num_segments=N_EXPERTS)
    padded = (counts + MOE_BLOCK - 1) // MOE_BLOCK * MOE_BLOCK
    grp_start = jnp.cumsum(counts) - counts
    pad_end = jnp.cumsum(padded)
    pad_start = pad_end - padded
    dest = pad_start[sorted_e] + jnp.arange(n_assign, dtype=jnp.int32) - grp_start[sorted_e]
    n_blocks = -(-n_assign // MOE_BLOCK) + N_EXPERTS
    n_rows = n_blocks * MOE_BLOCK
    row_tok = jnp.zeros((n_rows,), jnp.int32).at[dest].set((order // TOP_K).astype(jnp.int32))
    row_w = jnp.zeros((n_rows,), f32).at[dest].set(top_w.reshape(n_assign)[order])
    block_e = jnp.minimum(
        jnp.searchsorted(pad_end, jnp.arange(n_blocks, dtype=jnp.int32) * MOE_BLOCK, side='right'),
        N_EXPERTS - 1)

    def block_fn(blk):
        tok, wt, e = blk
        gu = t[tok] @ w_up[e] + b_up[e]
        glu = jnp.minimum(gu[:, :D_FF], SWIGLU_LIMIT)
        lin = jnp.clip(gu[:, D_FF:], -SWIGLU_LIMIT, SWIGLU_LIMIT)
        act = glu * jax.nn.sigmoid(SWIGLU_ALPHA * glu) * (lin + 1.0)
        yb = act @ w_down[e] + b_down[e]
        return yb.astype(f32) * wt[:, None]

    y_rows = lax.map(block_fn, (row_tok.reshape(n_blocks, MOE_BLOCK), row_w.reshape(n_blocks, MOE_BLOCK), block_e))
    out = jnp.zeros((n_tok, d), f32).at[row_tok].add(y_rows.reshape(n_rows, d))
    return out.reshape(bsz, seqlen, d).astype(h.dtype)


def _layer(x, cond, st_ssd, st_s5, st_dn, p):
    mod = (jax.nn.silu(cond) @ p['w_ada'] + p['b_ada']).reshape(cond.shape[0], 1, 6, D_MODEL)
    shift1, scale1, gate1 = mod[:, :, 0], mod[:, :, 1], mod[:, :, 2]
    shift2, scale2, gate2 = mod[:, :, 3], mod[:, :, 4], mod[:, :, 5]
    h = _rms_norm(x, p['norm1']) * (1.0 + scale1) + shift1
    mix, states = _mixing_block(h, st_ssd, st_s5, st_dn, p)
    x = (x + gate1 * mix).astype(x.dtype)
    h = _rms_norm(x, p['norm2']) * (1.0 + scale2) + shift2
    x = (x + gate2 * _moe(h, p['w_router'], p['b_router'], p['w_up'], p['b_up'], p['w_down'], p['b_down'])).astype(x.dtype)
    return x, states


def setup_inputs(seed: int = 0) -> dict:
    key = jax.random.key(seed)
    ks = iter(jax.random.split(key, 48))
    f32 = jnp.float32
    L = DEPTH

    def nrm(shape, scale):
        return jax.random.normal(next(ks), shape, f32) * scale

    def near_one(shape):
        return 1.0 + nrm(shape, 0.02)

    def uni(shape, lo, hi):
        return jax.random.uniform(next(ks), shape, f32, lo, hi)

    def inv_softplus_dt(shape):
        dt = jnp.exp(uni(shape, math.log(1e-3), math.log(1e-1)))
        return dt + jnp.log(-jnp.expm1(-dt))

    return {
        'x_prompt': nrm((BATCH, SEQ, D_MODEL), 1.0),
        'x_sample': nrm((DEC_BATCH, DEC_SEQ, D_MODEL), 1.0),
        'state_ssd': nrm((DEC_BATCH, L, 2, SSD_HEADS, SSD_HEAD_DIM, SSD_STATE), 0.1),
        'state_s5': nrm((DEC_BATCH, L, 2, 2, S5_GROUPS, S5_STATE), 0.1),
        'state_delta': nrm((DEC_BATCH, L, 2, DN_HEADS, DN_HEAD_DIM, DN_HEAD_DIM), 0.1),
        'c': nrm((DEC_BATCH, D_MODEL), 1.0),
        'c_ctx': nrm((D_MODEL,), 1.0),
        'norm1': near_one((L, D_MODEL)),
        'norm2': near_one((L, D_MODEL)),
        'w_ada': nrm((L, D_MODEL, 6 * D_MODEL), 0.5 * D_MODEL ** -0.5),
        'b_ada': nrm((L, 6 * D_MODEL), 0.01),
        'w_in': nrm((L, D_MODEL, IN_TOTAL), D_MODEL ** -0.5),
        'ssd_conv_w': nrm((L, CONV_K, SSD_XBC), CONV_K ** -0.5),
        'ssd_conv_b': nrm((L, SSD_XBC), 0.01),
        'ssd_a_log': jnp.log(uni((L, 2, SSD_HEADS), 1.0, 16.0)),
        'ssd_dt_bias': inv_softplus_dt((L, 2, SSD_HEADS)),
        'ssd_d': near_one((L, SSD_HEADS)),
        'ssd_norm': near_one((L, BRANCH_W)),
        's5_lam_re': -0.5 + nrm((L, 2, S5_GROUPS, S5_STATE), 0.01),
        's5_lam_im': math.pi * jnp.arange(S5_STATE, dtype=f32) + nrm((L, 2, S5_GROUPS, S5_STATE), 0.01),
        's5_log_step': uni((L, 2, S5_GROUPS), math.log(1e-3), math.log(1e-1)),
        's5_b_re': nrm((L, S5_GROUPS, S5_STATE, S5_GROUP_CH), (2 * S5_GROUP_CH) ** -0.5),
        's5_b_im': nrm((L, S5_GROUPS, S5_STATE, S5_GROUP_CH), (2 * S5_GROUP_CH) ** -0.5),
        's5_c_re': nrm((L, S5_GROUPS, S5_GROUP_CH, S5_STATE), S5_STATE ** -0.5),
        's5_c_im': nrm((L, S5_GROUPS, S5_GROUP_CH, S5_STATE), S5_STATE ** -0.5),
        's5_d': near_one((L, BRANCH_W)),
        's5_w_glu': nrm((L, BRANCH_W, 2 * BRANCH_W), BRANCH_W ** -0.5),
        'dn_conv_w': nrm((L, CONV_K, 3 * BRANCH_W), CONV_K ** -0.5),
        'dn_a_log': jnp.log(uni((L, 2, DN_HEADS), 1.0, 16.0)),
        'dn_dt_bias': inv_softplus_dt((L, 2, DN_HEADS)),
        'dn_norm': near_one((L, DN_HEAD_DIM)),
        'w_branch': nrm((L, N_BRANCH, BRANCH_W, D_MODEL), BRANCH_W ** -0.5),
        'w_out': nrm((L, D_MODEL, D_MODEL), D_MODEL ** -0.5),
        'w_router': nrm((L, D_MODEL, N_EXPERTS), D_MODEL ** -0.5),
        'b_router': nrm((L, N_EXPERTS), 0.01),
        'w_up': nrm((L, N_EXPERTS, D_MODEL, 2 * D_FF), D_MODEL ** -0.5),
        'b_up': nrm((L, N_EXPERTS, 2 * D_FF), 0.01),
        'w_down': nrm((L, N_EXPERTS, D_FF, D_MODEL), D_FF ** -0.5),
        'b_down': nrm((L, N_EXPERTS, D_MODEL), 0.01),
        'final_norm': near_one((D_MODEL,)),
    }


def reference(x_prompt, x_sample, state_ssd, state_s5, state_delta, c, c_ctx, norm1, norm2, w_ada, b_ada, w_in,
              ssd_conv_w, ssd_conv_b, ssd_a_log, ssd_dt_bias, ssd_d, ssd_norm, s5_lam_re, s5_lam_im, s5_log_step,
              s5_b_re, s5_b_im, s5_c_re, s5_c_im, s5_d, s5_w_glu, dn_conv_w, dn_a_log, dn_dt_bias, dn_norm,
              w_branch, w_out, w_router, b_router, w_up, b_up, w_down, b_down, final_norm):
    f32 = jnp.float32
    n_ctx = x_prompt.shape[0]
    zero_ssd = jnp.zeros((n_ctx, 2, SSD_HEADS, SSD_HEAD_DIM, SSD_STATE), f32)
    zero_s5 = jnp.zeros((n_ctx, 2, 2, S5_GROUPS, S5_STATE), f32)
    zero_dn = jnp.zeros((n_ctx, 2, DN_HEADS, DN_HEAD_DIM, DN_HEAD_DIM), f32)
    xc = x_prompt
    xs = x_sample + _grid_pos(x_sample.shape[1], x_sample.dtype)[None]
    cond_ctx = c_ctx[None, :]
    ssd_out, s5_out, dn_out = [], [], []
    for l in range(DEPTH):
        p = {
            'norm1': norm1[l], 'norm2': norm2[l], 'w_ada': w_ada[l], 'b_ada': b_ada[l], 'w_in': w_in[l],
            'ssd_conv_w': ssd_conv_w[l], 'ssd_conv_b': ssd_conv_b[l], 'ssd_a_log': ssd_a_log[l],
            'ssd_dt_bias': ssd_dt_bias[l], 'ssd_d': ssd_d[l], 'ssd_norm': ssd_norm[l],
            's5_lam_re': s5_lam_re[l], 's5_lam_im': s5_lam_im[l], 's5_log_step': s5_log_step[l],
            's5_b_re': s5_b_re[l], 's5_b_im': s5_b_im[l], 's5_c_re': s5_c_re[l], 's5_c_im': s5_c_im[l],
            's5_d': s5_d[l], 's5_w_glu': s5_w_glu[l],
            'dn_conv_w': dn_conv_w[l], 'dn_a_log': dn_a_log[l], 'dn_dt_bias': dn_dt_bias[l], 'dn_norm': dn_norm[l],
            'w_branch': w_branch[l], 'w_out': w_out[l],
            'w_router': w_router[l], 'b_router': b_router[l], 'w_up': w_up[l], 'b_up': b_up[l],
            'w_down': w_down[l], 'b_down': b_down[l],
        }
        xc, (s_a, s_b, s_c) = _layer(xc, cond_ctx, zero_ssd, zero_s5, zero_dn, p)
        ssd_out.append(s_a)
        s5_out.append(s_b)
        dn_out.append(s_c)
        xs, _ = _layer(xs, c, state_ssd[:, l], state_s5[:, l], state_delta[:, l], p)
    y_prompt = _rms_norm(xc, final_norm)
    y_sample = _rms_norm(xs, final_norm)
    new_state_ssd = jnp.stack(ssd_out, axis=1).astype(x_prompt.dtype)
    new_state_s5 = jnp.stack(s5_out, axis=1).astype(x_prompt.dtype)
    new_state_delta = jnp.stack(dn_out, axis=1).astype(x_prompt.dtype)
    return (y_prompt, y_sample, new_state_ssd, new_state_s5, new_state_delta)
```

```python
import functools

import numpy as np
import jax
import jax.numpy as jnp
from jax import lax
from jax.experimental import pallas as pl
from jax.experimental.pallas import tpu as pltpu

F32 = jnp.float32
BF16 = jnp.bfloat16
HIGHEST = lax.Precision.HIGHEST

ROW_BLOCK = 256
CHUNK = 64
LANES = 128
SUBLANES = 8
TOP_K = 4
NORM_EPS = 1e-6
L2_EPS = 1e-6
SWIGLU_ALPHA = 1.702
SWIGLU_LIMIT = 7.0
GRID_W = 64
NEG_BIG = -1e30
MIB = 1024 * 1024


def _params(sem, vmem_mib):
    return pltpu.CompilerParams(dimension_semantics=sem, vmem_limit_bytes=vmem_mib * MIB)


def _dot(a, b, precision=None):
    return jnp.dot(a, b, preferred_element_type=F32, precision=precision)


def _dot_nt(a, b):
    return lax.dot_general(a, b, (((1,), (1,)), ((), ())), preferred_element_type=F32)


def _dot_tn(a, b):
    return lax.dot_general(a, b, (((0,), (0,)), ((), ())), preferred_element_type=F32)


def _sigmoid(x):
    return 1.0 / (1.0 + jnp.exp(-x))


def _silu(x):
    return x * _sigmoid(x)


def _softplus(x):
    return jnp.maximum(x, 0.0) + jnp.log1p(jnp.exp(-jnp.abs(x)))


class _Layout:
    def __init__(self, n_ctx, l_ctx, n_smp, l_smp):
        self.n_ctx, self.l_ctx, self.n_smp, self.l_smp = n_ctx, l_ctx, n_smp, l_smp
        self.t_ctx = n_ctx * l_ctx
        self.t = self.t_ctx + n_smp * l_smp
        self.nblk = self.t // ROW_BLOCK
        self.nblk_ctx = self.t_ctx // ROW_BLOCK
        self.bps_ctx = l_ctx // ROW_BLOCK
        self.bps_smp = l_smp // ROW_BLOCK
        self.nseq = n_ctx + n_smp

    def flags(self, ib):
        is_ctx = ib < self.nblk_ctx
        j = jnp.where(is_ctx, ib, ib - self.nblk_ctx)
        bps = jnp.where(is_ctx, self.bps_ctx, self.bps_smp)
        pos = lax.rem(j, bps)
        seq = jnp.where(is_ctx, lax.div(j, bps), self.n_ctx + lax.div(j, bps))
        return is_ctx, pos == 0, pos == bps - 1, seq

    def smp_seq(self, ib):
        j = jnp.maximum(ib - self.nblk_ctx, 0)
        return lax.div(j, self.bps_smp)

    def cond_row(self, i, rows):
        n_ctx_tiles = self.t_ctx // rows
        return jnp.where(i < n_ctx_tiles, 0, 1 + lax.div(jnp.maximum(i - n_ctx_tiles, 0), self.l_smp // rows))


def _ada_body(c_ref, w_ref, b_ref, o_ref):
    c = c_ref[...]
    o_ref[...] = _dot(_silu(c).astype(BF16), w_ref[...].astype(BF16)) + b_ref[...]


def _ada_call(cond, w_ada, b_ada):
    depth, d, n = w_ada.shape
    rows = cond.shape[0]
    tn = 1024
    return pl.pallas_call(
        _ada_body,
        grid=(depth, n // tn),
        in_specs=[
            pl.BlockSpec((rows, d), lambda l, j: (0, 0)),
            pl.BlockSpec((None, d, tn), lambda l, j: (l, 0, j)),
            pl.BlockSpec((None, 1, tn), lambda l, j: (l, 0, j)),
        ],
        out_specs=pl.BlockSpec((None, rows, tn), lambda l, j: (l, 0, j)),
        out_shape=jax.ShapeDtypeStruct((depth, rows, n), F32),
        compiler_params=_params(("arbitrary", "arbitrary"), 40),
        name="ada_mod",
    )(cond, w_ada, b_ada.reshape(depth, 1, n))


def _modulated_norm(x, nw, shift, scale):
    ms = jnp.mean(x * x, axis=-1, keepdims=True)
    y = x * lax.rsqrt(ms + NORM_EPS) * nw
    return y * (1.0 + scale) + shift


def _in_proj_body(x_ref, nw_ref, mod_ref, w_ref, o_ref, h_ref):
    @pl.when(pl.program_id(1) == 0)
    def _():
        h = _modulated_norm(x_ref[...], nw_ref[...], mod_ref[0:1, :], mod_ref[1:2, :])
        h_ref[...] = h.astype(BF16)

    o_ref[...] = _dot(h_ref[...], w_ref[...])


def _in_proj_call(lay, l, x, norm_w, mod, w_in):
    t, d = x.shape
    n = w_in.shape[-1]
    tm = min(1024, lay.l_smp)
    tn = 1024
    return pl.pallas_call(
        _in_proj_body,
        grid=(t // tm, n // tn),
        in_specs=[
            pl.BlockSpec((tm, d), lambda i, j: (i, 0)),
            pl.BlockSpec((None, 1, d), lambda i, j: (l, 0, 0)),
            pl.BlockSpec((None, None, 6, d), lambda i, j: (l, lay.cond_row(i, tm), 0, 0)),
            pl.BlockSpec((None, d, tn), lambda i, j: (l, 0, j)),
        ],
        out_specs=pl.BlockSpec((tm, tn), lambda i, j: (i, j)),
        out_shape=jax.ShapeDtypeStruct((t, n), F32),
        scratch_shapes=[pltpu.VMEM((tm, d), BF16)],
        compiler_params=_params(("arbitrary", "arbitrary"), 48),
        name="in_proj",
    )(x, norm_w, mod, w_in)


def _conv_body(x_ref, p_ref, n_ref, w_ref, b_ref, o_ref, ext_ref, *, lay, n_qk_blocks, conv_k):
    i = pl.program_id(0)
    c = pl.program_id(1)
    _, first, last, _ = lay.flags(i)
    halo = SUBLANES
    ext_ref[0:halo, :] = jnp.where(first, 0.0, p_ref[...])
    ext_ref[halo:halo + ROW_BLOCK, :] = x_ref[...]
    ext_ref[halo + ROW_BLOCK:2 * halo + ROW_BLOCK, :] = jnp.where(last, 0.0, n_ref[...])
    acc = b_ref[...] + jnp.zeros((ROW_BLOCK, x_ref.shape[1]), F32)
    for k in range(conv_k):
        off = halo - conv_k // 2 + k
        acc = acc + w_ref[k:k + 1, :] * ext_ref[off:off + ROW_BLOCK, :]
    y = _silu(acc)

    @pl.when(c < n_qk_blocks)
    def _():
        for hs in range(y.shape[1] // LANES):
            yh = y[:, hs * LANES:(hs + 1) * LANES]
            ss = jnp.sum(yh * yh, axis=-1, keepdims=True)
            o_ref[:, hs * LANES:(hs + 1) * LANES] = yh * lax.rsqrt(ss + L2_EPS)

    @pl.when(c >= n_qk_blocks)
    def _():
        o_ref[...] = y


def _conv_call(lay, l, proj, conv_w, conv_b, col0, width, qk_width):
    t = proj.shape[0]
    cb = 512
    conv_k = conv_w.shape[1]
    off = col0 // cb
    rb8 = ROW_BLOCK // SUBLANES
    body = functools.partial(_conv_body, lay=lay, n_qk_blocks=qk_width // cb, conv_k=conv_k)
    return pl.pallas_call(
        body,
        grid=(lay.nblk, width // cb),
        in_specs=[
            pl.BlockSpec((ROW_BLOCK, cb), lambda i, c: (i, off + c)),
            pl.BlockSpec((SUBLANES, cb), lambda i, c: (jnp.maximum(i * rb8 - 1, 0), off + c)),
            pl.BlockSpec((SUBLANES, cb), lambda i, c: (jnp.minimum((i + 1) * rb8, t // SUBLANES - 1), off + c)),
            pl.BlockSpec((None, conv_k, cb), lambda i, c: (l, 0, c)),
            pl.BlockSpec((None, 1, cb), lambda i, c: (l, 0, c)),
        ],
        out_specs=pl.BlockSpec((ROW_BLOCK, cb), lambda i, c: (i, c)),
        out_shape=jax.ShapeDtypeStruct((t, width), F32),
        scratch_shapes=[pltpu.VMEM((ROW_BLOCK + 2 * SUBLANES, cb), F32)],
        compiler_params=_params(("arbitrary", "arbitrary"), 24),
        name="dwconv_silu",
    )(proj, proj, proj, conv_w, conv_b)


def _chunk_consts(width):
    r = np.arange(ROW_BLOCK)
    same = (r[:, None] // CHUNK) == (r[None, :] // CHUNK)
    tri = np.stack([same & (r[None, :] <= r[:, None]), same & (r[None, :] >= r[:, None])]).astype(np.float32)
    ones_bd = same.astype(np.float32)
    s = np.arange(width) % CHUNK
    rr = r % CHUNK
    mask = np.stack([rr[:, None] >= s[None, :], rr[:, None] <= s[None, :]]).astype(np.float32)
    eye = (rr[:, None] == s[None, :]).astype(np.float32)
    return jnp.asarray(tri), jnp.asarray(ones_bd), jnp.asarray(mask), jnp.asarray(eye)


def _expand_matrix(col_of_head, n_heads, lanes_per_head):
    e = np.zeros((2, LANES, n_heads * lanes_per_head), np.float32)
    for d in range(2):
        for h in range(n_heads):
            e[d, col_of_head(d, h), h * lanes_per_head:(h + 1) * lanes_per_head] = 1.0
    return jnp.asarray(e)


def _half_masks():
    lane = lax.broadcasted_iota(jnp.int32, (1, LANES), 1)
    m0 = (lane < CHUNK).astype(F32)
    return m0, 1.0 - m0


def _ssd_body(x_ref, bc_ref, sm_ref, e_ref, bias_ref, aneg_ref, tri_ref, ones_ref, mask_ref, eye_ref, s0_ref,
              y_ref, sout_ref, st_ref, cs_ref, ld_ref, xd_ref, *, lay, n_groups, n_state, heads_per_group):
    d = pl.program_id(0)
    i = pl.program_id(1)
    ib = jnp.where(d == 0, i, lay.nblk - 1 - i)
    is_ctx, first, last, _ = lay.flags(ib)
    start = jnp.where(d == 0, first, last)
    end = jnp.where(d == 0, last, first)

    @pl.when(start & is_ctx)
    def _():
        st_ref[...] = jnp.zeros_like(st_ref)

    @pl.when(start & jnp.logical_not(is_ctx))
    def _():
        st_ref[...] = s0_ref[...]

    dt = _dot(_softplus(sm_ref[...] + bias_ref[...]), e_ref[...], HIGHEST)
    a = dt * aneg_ref[...]
    cs = _dot(tri_ref[...], a, HIGHEST)
    cs_cols = _dot(ones_ref[...], eye_ref[...] * cs, HIGHEST)
    cs_ref[...] = cs
    ld_ref[...] = jnp.exp(jnp.where(mask_ref[...] > 0.5, cs - cs_cols, NEG_BIG))
    xd_ref[...] = x_ref[...] * dt

    m0, m1 = _half_masks()
    n_chunks = ROW_BLOCK // CHUNK
    gw = heads_per_group * CHUNK
    for k in range(n_chunks):
        c = jnp.where(d == 0, k, n_chunks - 1 - k)
        rows = pl.ds(pl.multiple_of(c * CHUNK, CHUNK), CHUNK)
        cs_c = cs_ref[rows, :]
        ld_c = ld_ref[rows, :]
        xd_c = xd_ref[rows, :]
        bc_c = bc_ref[rows, :]
        cs_tot = jnp.where(d == 0, cs_c[CHUNK - 1:CHUNK, :], cs_c[0:1, :])
        ecs = jnp.exp(cs_c)
        etot = jnp.exp(cs_tot)
        xdec = xd_c * jnp.exp(cs_tot - cs_c)
        for g in range(n_groups):
            bg = bc_c[:, g * n_state:(g + 1) * n_state].astype(BF16)
            cg = bc_c[:, (n_groups + g) * n_state:(n_groups + g + 1) * n_state].astype(BF16)
            cb2 = _dot_nt(cg, jnp.concatenate([bg, bg], axis=0))
            lo = g * gw
            st_g = st_ref[:, lo:lo + gw]
            y_off = _dot(cg, st_g.astype(BF16)) * ecs[:, lo:lo + gw]
            y_diag = []
            for jj in range(heads_per_group // 2):
                sl = slice(lo + jj * LANES, lo + (jj + 1) * LANES)
                m = (cb2 * ld_c[:, sl]).astype(BF16)
                xj = xd_c[:, sl]
                bd = jnp.concatenate([xj * m0, xj * m1], axis=0).astype(BF16)
                y_diag.append(_dot(m, bd))
            y_ref[rows, lo:lo + gw] = jnp.concatenate(y_diag, axis=1) + y_off
            st_ref[:, lo:lo + gw] = etot[:, lo:lo + gw] * st_g + _dot_tn(bg, xdec[:, lo:lo + gw].astype(BF16))

    @pl.when(end)
    def _():
        sout_ref[...] = st_ref[...]


def _ssd_call(lay, l, conv, proj, small_col, x_col, bc_col, consts, e_dt, bias, aneg, state_t):
    t = conv.shape[0]
    w = aneg.shape[-1]
    n_state = state_t.shape[-2]
    bcw = conv.shape[1] - x_col - w
    n_groups = bcw // (2 * n_state)
    heads = w // CHUNK
    tri, ones_bd, mask, eye = consts
    rev = lambda d, i: jnp.where(d == 0, i, lay.nblk - 1 - i)
    body = functools.partial(_ssd_body, lay=lay, n_groups=n_groups, n_state=n_state,
                             heads_per_group=heads // n_groups)
    return pl.pallas_call(
        body,
        grid=(2, lay.nblk),
        in_specs=[
            pl.BlockSpec((ROW_BLOCK, w), lambda d, i: (rev(d, i), x_col // w)),
            pl.BlockSpec((ROW_BLOCK, bcw), lambda d, i: (rev(d, i), bc_col // bcw)),
            pl.BlockSpec((ROW_BLOCK, LANES), lambda d, i: (rev(d, i), small_col // LANES)),
            pl.BlockSpec((None, LANES, w), lambda d, i: (d, 0, 0)),
            pl.BlockSpec((None, 1, LANES), lambda d, i: (l, 0, 0)),
            pl.BlockSpec((None, None, 1, w), lambda d, i: (l, d, 0, 0)),
            pl.BlockSpec((None, ROW_BLOCK, ROW_BLOCK), lambda d, i: (d, 0, 0)),
            pl.BlockSpec((ROW_BLOCK, ROW_BLOCK), lambda d, i: (0, 0)),
            pl.BlockSpec((None, ROW_BLOCK, w), lambda d, i: (d, 0, 0)),
            pl.BlockSpec((ROW_BLOCK, w), lambda d, i: (0, 0)),
            pl.BlockSpec((None, None, None, n_state, w), lambda d, i: (lay.smp_seq(rev(d, i)), l, d, 0, 0)),
        ],
        out_specs=[
            pl.BlockSpec((None, ROW_BLOCK, w), lambda d, i: (d, rev(d, i), 0)),
            pl.BlockSpec((None, None, n_state, w), lambda d, i: (lay.flags(rev(d, i))[3], d, 0, 0)),
        ],
        out_shape=[
            jax.ShapeDtypeStruct((2, t, w), F32),
            jax.ShapeDtypeStruct((lay.nseq, 2, n_state, w), F32),
        ],
        scratch_shapes=[pltpu.VMEM((n_state, w), F32)] + [pltpu.VMEM((ROW_BLOCK, w), F32)] * 3,
        compiler_params=_params(("arbitrary", "arbitrary"), 40),
        name="ssd_scan",
    )(conv, conv, proj, e_dt, bias, aneg, tri, ones_bd, mask, eye, state_t)


def _s5_body(u_ref, bbr_ref, bbi_ref, cr_ref, ci_ref, ar_ref, ai_ref, h0_ref,
             y_ref, hout_ref, bur_ref, bui_ref, hr_ref, hi_ref, *, nblk, nblk_ctx, bpg_ctx, bpg_smp):
    d = pl.program_id(0)
    i = pl.program_id(1)
    ib = jnp.where(d == 0, i, nblk - 1 - i)
    is_ctx = ib < nblk_ctx
    j = jnp.where(is_ctx, ib, ib - nblk_ctx)
    bpg = jnp.where(is_ctx, bpg_ctx, bpg_smp)
    pos = lax.rem(j, bpg)
    start = jnp.where(d == 0, pos == 0, pos == bpg - 1)
    end = jnp.where(d == 0, pos == bpg - 1, pos == 0)

    @pl.when(start & is_ctx)
    def _():
        hr_ref[...] = jnp.zeros_like(hr_ref)
        hi_ref[...] = jnp.zeros_like(hi_ref)

    @pl.when(start & jnp.logical_not(is_ctx))
    def _():
        hr_ref[...] = h0_ref[0]
        hi_ref[...] = h0_ref[1]

    n_lane_chunks = u_ref.shape[1] // LANES
    sw = bbr_ref.shape[-1]
    for cc in range(n_lane_chunks):
        uc = u_ref[:, cc * LANES:(cc + 1) * LANES].astype(BF16)
        bur_ref[:, cc * sw:(cc + 1) * sw] = _dot(uc, bbr_ref[cc])
        bui_ref[:, cc * sw:(cc + 1) * sw] = _dot(uc, bbi_ref[cc])

    n_steps = ROW_BLOCK // SUBLANES
    n_state = bur_ref.shape[1]
    seg = 512

    def step(s, carry):
        tstep = jnp.where(d == 0, s, n_steps - 1 - s)
        rows = pl.ds(pl.multiple_of(tstep * SUBLANES, SUBLANES), SUBLANES)
        for q in range(n_state // seg):
            sl = slice(q * seg, (q + 1) * seg)
            hr = hr_ref[:, sl]
            hi = hi_ref[:, sl]
            ar = ar_ref[:, sl]
            ai = ai_ref[:, sl]
            nr = ar * hr - ai * hi + bur_ref[rows, sl]
            ni = ar * hi + ai * hr + bui_ref[rows, sl]
            hr_ref[:, sl] = nr
            hi_ref[:, sl] = ni
            bur_ref[rows, sl] = nr
            bui_ref[rows, sl] = ni
        return carry

    lax.fori_loop(0, n_steps, step, 0)

    for cc in range(n_lane_chunks):
        hr = bur_ref[:, cc * sw:(cc + 1) * sw].astype(BF16)
        hi = bui_ref[:, cc * sw:(cc + 1) * sw].astype(BF16)
        y_ref[:, cc * LANES:(cc + 1) * LANES] = _dot(hr, cr_ref[cc]) - _dot(hi, ci_ref[cc])

    @pl.when(end)
    def _():
        hout_ref[0] = hr_ref[...]
        hout_ref[1] = hi_ref[...]


def _s5_call(lay, l, u_tm, bb_re, bb_im, c_re, c_im, abar_re, abar_im, h0):
    t, w = u_tm.shape
    n_state = abar_re.shape[-1]
    nch = w // LANES
    sw = n_state // nch
    ng_ctx = lay.n_ctx // SUBLANES
    ng_smp = lay.n_smp // SUBLANES
    bpg_ctx = lay.l_ctx * SUBLANES // ROW_BLOCK
    bpg_smp = lay.l_smp * SUBLANES // ROW_BLOCK
    rev = lambda d, i: jnp.where(d == 0, i, lay.nblk - 1 - i)

    def group(ib):
        is_ctx = ib < lay.nblk_ctx
        return jnp.where(is_ctx, lax.div(ib, bpg_ctx), ng_ctx + lax.div(jnp.maximum(ib - lay.nblk_ctx, 0), bpg_smp))

    def smp_group(ib):
        return lax.div(jnp.maximum(ib - lay.nblk_ctx, 0), bpg_smp)

    body = functools.partial(_s5_body, nblk=lay.nblk, nblk_ctx=lay.nblk_ctx, bpg_ctx=bpg_ctx, bpg_smp=bpg_smp)
    return pl.pallas_call(
        body,
        grid=(2, lay.nblk),
        in_specs=[
            pl.BlockSpec((ROW_BLOCK, w), lambda d, i: (rev(d, i), 0)),
            pl.BlockSpec((None, None, nch, LANES, sw), lambda d, i: (l, d, 0, 0, 0)),
            pl.BlockSpec((None, None, nch, LANES, sw), lambda d, i: (l, d, 0, 0, 0)),
            pl.BlockSpec((None, nch, sw, LANES), lambda d, i: (l, 0, 0, 0)),
            pl.BlockSpec((None, nch, sw, LANES), lambda d, i: (l, 0, 0, 0)),
            pl.BlockSpec((None, None, 1, n_state), lambda d, i: (l, d, 0, 0)),
            pl.BlockSpec((None, None, 1, n_state), lambda d, i: (l, d, 0, 0)),
            pl.BlockSpec((None, None, None, 2, SUBLANES, n_state),
                         lambda d, i: (smp_group(rev(d, i)), l, d, 0, 0, 0)),
        ],
        out_specs=[
            pl.BlockSpec((None, ROW_BLOCK, w), lambda d, i: (d, rev(d, i), 0)),
            pl.BlockSpec((None, None, 2, SUBLANES, n_state), lambda d, i: (group(rev(d, i)), d, 0, 0, 0)),
        ],
        out_shape=[
            jax.ShapeDtypeStruct((2, t, w), F32),
            jax.ShapeDtypeStruct((ng_ctx + ng_smp, 2, 2, SUBLANES, n_state), F32),
        ],
        scratch_shapes=[pltpu.VMEM((ROW_BLOCK, n_state), F32)] * 2 + [pltpu.VMEM((SUBLANES, n_state), F32)] * 2,
        compiler_params=_params(("arbitrary", "arbitrary"), 40),
        name="s5_scan",
    )(u_tm, bb_re, bb_im, c_re, c_im, abar_re, abar_im, h0)


def _dn_body(q_ref, k_ref, v_ref, sm_ref, eb_ref, eg_ref, egc_ref, bias_ref, nega_ref, negac_ref,
             tri_ref, ones_ref, mask_ref, eye_ref, s0_ref,
             o_ref, sout_ref, s_ref, u_ref, w_ref, at_ref, eg_s, gc_s, *, lay, n_heads):
    d = pl.program_id(0)
    i = pl.program_id(1)
    ib = jnp.where(d == 0, i, lay.nblk - 1 - i)
    is_ctx, first, last, _ = lay.flags(ib)
    start = jnp.where(d == 0, first, last)
    end = jnp.where(d == 0, last, first)

    @pl.when(start & is_ctx)
    def _():
        s_ref[...] = jnp.zeros_like(s_ref)

    @pl.when(start & jnp.logical_not(is_ctx))
    def _():
        s_ref[...] = s0_ref[...]

    dk = LANES
    scale = dk ** -0.5
    sm = sm_ref[...]
    sp = _softplus(sm + bias_ref[...])
    beta = _dot(_sigmoid(sm), eb_ref[...], HIGHEST)
    g_wide = _dot(sp, eg_ref[...], HIGHEST) * nega_ref[...]
    g_cols = _dot(sp, egc_ref[...], HIGHEST) * negac_ref[...]
    gc_wide = _dot(tri_ref[...], g_wide, HIGHEST)
    gc_cols = _dot(tri_ref[...], g_cols, HIGHEST)
    gc_t = _dot(ones_ref[...], eye_ref[...] * gc_cols, HIGHEST)
    incl = mask_ref[...]
    dec = jnp.exp(jnp.where(incl > 0.5, gc_cols - gc_t, NEG_BIG))
    strict = incl - eye_ref[...]
    eg = jnp.exp(gc_wide)
    eg_s[...] = eg
    gc_s[...] = gc_wide

    m0, m1 = _half_masks()
    lane2 = lax.broadcasted_iota(jnp.int32, (1, 2 * LANES), 1)
    ml = (lane2 < LANES).astype(F32)
    mr = 1.0 - ml
    ri = lax.broadcasted_iota(jnp.int32, (LANES, LANES), 0)
    ci = lax.broadcasted_iota(jnp.int32, (LANES, LANES), 1)
    ident = (ri == ci).astype(F32)
    same = lambda s: lax.shift_right_logical(ri, s) == lax.shift_right_logical(ci, s)
    diag16 = same(4).astype(F32)
    off32 = (same(5) & jnp.logical_not(same(4))).astype(F32)
    off64 = (same(6) & jnp.logical_not(same(5))).astype(F32)
    n_chunks = ROW_BLOCK // CHUNK
    n_pairs = n_heads // 2

    for k in range(n_chunks):
        rs = slice(k * CHUNK, (k + 1) * CHUNK)
        for j in range(n_pairs):
            la = slice(2 * j * LANES, (2 * j + 1) * LANES)
            lb = slice((2 * j + 1) * LANES, (2 * j + 2) * LANES)
            lp = slice(j * LANES, (j + 1) * LANES)
            k_a, k_b = k_ref[rs, la], k_ref[rs, lb]
            kb_a, kb_b = k_a * beta[rs, la], k_b * beta[rs, lb]
            kk = jnp.concatenate([k_a, k_b], axis=1)
            bdk = jnp.concatenate([kk * ml, kk * mr], axis=0).astype(BF16)
            kbk = _dot_nt(jnp.concatenate([kb_a, kb_b], axis=1).astype(BF16), bdk)
            dec_p = dec[rs, lp]
            a_p = kbk * dec_p * strict[rs, lp]
            a_bd = jnp.concatenate([a_p * m0, a_p * m1], axis=0)
            p = -(a_bd * diag16)
            tinv = ident + p
            for _ in range(3):
                p = _dot(p, p, HIGHEST)
                tinv = tinv + _dot(tinv, p, HIGHEST)
            for off in (off32, off64):
                tinv = tinv - _dot(_dot(tinv, a_bd * off, HIGHEST), tinv, HIGHEST)
            rhs = jnp.concatenate([
                jnp.concatenate([v_ref[rs, la] * beta[rs, la], kb_a * eg[rs, la]], axis=1),
                jnp.concatenate([v_ref[rs, lb] * beta[rs, lb], kb_b * eg[rs, lb]], axis=1)], axis=0)
            sol = _dot(tinv, rhs, HIGHEST)
            u_ref[k, j] = sol[:, :LANES]
            w_ref[k, j] = sol[:, LANES:]
            qq = jnp.concatenate([q_ref[rs, la], q_ref[rs, lb]], axis=1) * scale
            att = _dot_nt(qq.astype(BF16), bdk) * dec_p
            at_ref[k, j] = jnp.concatenate([att * m0, att * m1], axis=0)

    for kk_ in range(n_chunks):
        c = jnp.where(d == 0, kk_, n_chunks - 1 - kk_)
        rows = pl.ds(pl.multiple_of(c * CHUNK, CHUNK), CHUNK)
        gc_c = gc_s[rows, :]
        eg_c = eg_s[rows, :]
        gc_tot = jnp.where(d == 0, gc_c[CHUNK - 1:CHUNK, :], gc_c[0:1, :])
        g_tot = jnp.exp(gc_tot)
        kdec = jnp.exp(gc_tot - gc_c)
        for j in range(n_pairs):
            u_p = u_ref[c, j]
            w_p = w_ref[c, j]
            vnew = []
            for hh in range(2):
                h = 2 * j + hh
                s_h = s_ref[h].astype(BF16)
                hr = slice(hh * CHUNK, (hh + 1) * CHUNK)
                vnew.append(u_p[hr, :] - _dot(w_p[hr, :].astype(BF16), s_h))
            o_p = _dot(at_ref[c, j].astype(BF16), jnp.concatenate(vnew, axis=0).astype(BF16))
            for hh in range(2):
                h = 2 * j + hh
                lh = slice(h * LANES, (h + 1) * LANES)
                hr = slice(hh * CHUNK, (hh + 1) * CHUNK)
                s_h = s_ref[h]
                qd = q_ref[rows, lh] * scale * eg_c[:, lh]
                o_ref[rows, lh] = _dot(qd.astype(BF16), s_h.astype(BF16)) + o_p[hr, :]
                kd = k_ref[rows, lh] * kdec[:, lh]
                s_ref[h] = s_h * g_tot[:, lh] + _dot_tn(kd.astype(BF16), vnew[hh].astype(BF16))

    @pl.when(end)
    def _():
        sout_ref[...] = s_ref[...]


def _dn_call(lay, l, conv, proj, small_col, consts, e_beta, e_g, e_gc, bias, nega, negac, state):
    t = conv.shape[0]
    w = nega.shape[-1]
    n_heads = w // LANES
    wc = n_heads * CHUNK
    tri, ones_bd, mask, eye = consts
    n_chunks = ROW_BLOCK // CHUNK
    rev = lambda d, i: jnp.where(d == 0, i, lay.nblk - 1 - i)
    body = functools.partial(_dn_body, lay=lay, n_heads=n_heads)
    return pl.pallas_call(
        body,
        grid=(2, lay.nblk),
        in_specs=[
            pl.BlockSpec((ROW_BLOCK, w), lambda d, i: (rev(d, i), 0)),
            pl.BlockSpec((ROW_BLOCK, w), lambda d, i: (rev(d, i), 1)),
            pl.BlockSpec((ROW_BLOCK, w), lambda d, i: (rev(d, i), 2)),
            pl.BlockSpec((ROW_BLOCK, LANES), lambda d, i: (rev(d, i), small_col // LANES)),
            pl.BlockSpec((None, LANES, w), lambda d, i: (d, 0, 0)),
            pl.BlockSpec((None, LANES, w), lambda d, i: (d, 0, 0)),
            pl.BlockSpec((None, LANES, wc), lambda d, i: (d, 0, 0)),
            pl.BlockSpec((None, 1, LANES), lambda d, i: (l, 0, 0)),
            pl.BlockSpec((None, None, 1, w), lambda d, i: (l, d, 0, 0)),
            pl.BlockSpec((None, None, 1, wc), lambda d, i: (l, d, 0, 0)),
            pl.BlockSpec((None, ROW_BLOCK, ROW_BLOCK), lambda d, i: (d, 0, 0)),
            pl.BlockSpec((ROW_BLOCK, ROW_BLOCK), lambda d, i: (0, 0)),
            pl.BlockSpec((None, ROW_BLOCK, wc), lambda d, i: (d, 0, 0)),
            pl.BlockSpec((ROW_BLOCK, wc), lambda d, i: (0, 0)),
            pl.BlockSpec((None, None, None, n_heads, LANES, LANES),
                         lambda d, i: (lay.smp_seq(rev(d, i)), l, d, 0, 0, 0)),
        ],
        out_specs=[
            pl.BlockSpec((None, ROW_BLOCK, w), lambda d, i: (d, rev(d, i), 0)),
            pl.BlockSpec((None, None, n_heads, LANES, LANES), lambda d, i: (lay.flags(rev(d, i))[3], d, 0, 0, 0)),
        ],
        out_shape=[
            jax.ShapeDtypeStruct((2, t, w), F32),
            jax.ShapeDtypeStruct((lay.nseq, 2, n_heads, LANES, LANES), F32),
        ],
        scratch_shapes=[pltpu.VMEM((n_heads, LANES, LANES), F32)]
        + [pltpu.VMEM((n_chunks, n_heads // 2, LANES, LANES), F32)] * 3
        + [pltpu.VMEM((ROW_BLOCK, w), F32)] * 2,
        compiler_params=_params(("arbitrary", "arbitrary"), 40),
        name="deltanet_scan",
    )(conv, conv, conv, proj, e_beta, e_g, e_gc, bias, nega, negac, tri, ones_bd, mask, eye, state)


def _gelu_tanh(x):
    return 0.5 * x * (1.0 + jnp.tanh(np.sqrt(2.0 / np.pi).astype(np.float32) * (x + 0.044715 * x * x * x)))


def _branch_body(ys_ref, xc_ref, z_ref, y5_ref, u_ref, od_ref, g_ref, dexp_ref, nssd_ref, s5d_ref, ndn_ref,
                 wglu_ref, o_ref):
    w = z_ref.shape[1]
    ya = (ys_ref[0] + ys_ref[1] + dexp_ref[...] * xc_ref[...]) * _silu(z_ref[...])
    ya = ya * lax.rsqrt(jnp.mean(ya * ya, axis=-1, keepdims=True) + NORM_EPS) * nssd_ref[...]
    o_ref[0] = ya.astype(BF16)

    yb = _gelu_tanh(y5_ref[0] + y5_ref[1] + s5d_ref[...] * u_ref[...])
    glu = _dot(yb.astype(BF16), wglu_ref[...])
    o_ref[1] = (glu[:, :w] * _sigmoid(glu[:, w:])).astype(BF16)

    gate = _silu(g_ref[...])
    for h in range(w // LANES):
        sl = slice(h * LANES, (h + 1) * LANES)
        o = od_ref[0, :, sl] + od_ref[1, :, sl]
        o = o * lax.rsqrt(jnp.mean(o * o, axis=-1, keepdims=True) + NORM_EPS) * ndn_ref[:, sl]
        o_ref[2, :, sl] = (o * gate[:, sl]).astype(BF16)


def _branch_call(lay, l, y_ssd, conv, x_col, proj, z_col, u_col, g_col, y_s5, o_dn, d_exp, n_ssd, s5_d, n_dn, w_glu):
    t = conv.shape[0]
    w = d_exp.shape[-1]
    vec = pl.BlockSpec((None, 1, w), lambda i: (l, 0, 0))
    pair = pl.BlockSpec((2, ROW_BLOCK, w), lambda i: (0, i, 0))
    return pl.pallas_call(
        _branch_body,
        grid=(lay.nblk,),
        in_specs=[
            pair,
            pl.BlockSpec((ROW_BLOCK, w), lambda i: (i, x_col // w)),
            pl.BlockSpec((ROW_BLOCK, w), lambda i: (i, z_col // w)),
            pair,
            pl.BlockSpec((ROW_BLOCK, w), lambda i: (i, u_col // w)),
            pair,
            pl.BlockSpec((ROW_BLOCK, w), lambda i: (i, g_col // w)),
            vec, vec, vec, vec,
            pl.BlockSpec((None, w, 2 * w), lambda i: (l, 0, 0)),
        ],
        out_specs=pl.BlockSpec((3, ROW_BLOCK, w), lambda i: (0, i, 0)),
        out_shape=jax.ShapeDtypeStruct((3, t, w), BF16),
        compiler_params=_params(("arbitrary",), 48),
        name="branch_epilogue",
    )(y_ssd, conv, proj, y_s5, proj, o_dn, proj, d_exp, n_ssd, s5_d, n_dn, w_glu)


def _merge_body(ys_ref, g0_ref, g1_ref, g2_ref, wb_ref, o_ref):
    acc = _sigmoid(g0_ref[...]) * _dot(ys_ref[0], wb_ref[0])
    acc = acc + _sigmoid(g1_ref[...]) * _dot(ys_ref[1], wb_ref[1])
    acc = acc + _sigmoid(g2_ref[...]) * _dot(ys_ref[2], wb_ref[2])
    o_ref[...] = acc.astype(BF16)


def _merge_call(lay, l, ys, proj, gate_col, w_branch):
    _, t, w = ys.shape
    d = w_branch.shape[-1]
    g0 = gate_col // d
    gates = [pl.BlockSpec((ROW_BLOCK, d), lambda i, k=k: (i, g0 + k)) for k in range(3)]
    return pl.pallas_call(
        _merge_body,
        grid=(lay.nblk,),
        in_specs=[pl.BlockSpec((3, ROW_BLOCK, w), lambda i: (0, i, 0))] + gates
        + [pl.BlockSpec((None, 3, w, d), lambda i: (l, 0, 0, 0))],
        out_specs=pl.BlockSpec((ROW_BLOCK, d), lambda i: (i, 0)),
        out_shape=jax.ShapeDtypeStruct((t, d), BF16),
        compiler_params=_params(("arbitrary",), 52),
        name="branch_merge",
    )(ys, proj, proj, proj, w_branch)


def _out_proj_body(m_ref, w_ref, x_ref, mod_ref, o_ref):
    o_ref[...] = x_ref[...] + mod_ref[2:3, :] * _dot(m_ref[...], w_ref[...])


def _out_proj_call(lay, l, merged, w_out, x, mod):
    t, d = x.shape
    tm = min(512, lay.l_smp)
    return pl.pallas_call(
        _out_proj_body,
        grid=(t // tm,),
        in_specs=[
            pl.BlockSpec((tm, d), lambda i: (i, 0)),
            pl.BlockSpec((None, d, d), lambda i: (l, 0, 0)),
            pl.BlockSpec((tm, d), lambda i: (i, 0)),
            pl.BlockSpec((None, None, 6, d), lambda i: (l, lay.cond_row(i, tm), 0, 0)),
        ],
        out_specs=pl.BlockSpec((tm, d), lambda i: (i, 0)),
        out_shape=jax.ShapeDtypeStruct((t, d), F32),
        compiler_params=_params(("arbitrary",), 48),
        name="out_proj",
    )(merged, w_out, x, mod)


def _router_body(x_ref, nw_ref, mod_ref, wr_ref, br_ref, lt_ref,
                 h_ref, e_ref, p_ref, r_ref, cnt_ref, run_ref, *, n_experts):
    i = pl.program_id(0)

    @pl.when(i == 0)
    def _():
        run_ref[...] = jnp.zeros_like(run_ref)

    h = _modulated_norm(x_ref[...], nw_ref[...], mod_ref[3:4, :], mod_ref[4:5, :])
    h_ref[...] = h
    logits = _dot(h, wr_ref[...], HIGHEST) + br_ref[...]
    rows = logits.shape[0]
    lane = lax.broadcasted_iota(jnp.int32, (rows, n_experts), 1).astype(F32)
    out_lane = lax.broadcasted_iota(jnp.int32, (rows, LANES), 1)
    cur = logits
    sels, tops, idxs = [], [], []
    for _ in range(TOP_K):
        m = jnp.max(cur, axis=-1, keepdims=True)
        idx = jnp.min(jnp.where(cur == m, lane, float(n_experts)), axis=-1, keepdims=True)
        sel = lane == idx
        sels.append(sel)
        tops.append(m)
        idxs.append(idx)
        cur = jnp.where(sel, -jnp.inf, cur)
    exps = [jnp.exp(m - tops[0]) for m in tops]
    denom = exps[0] + exps[1] + exps[2] + exps[3]
    onehot = jnp.zeros((rows, n_experts), F32)
    for sel in sels:
        onehot = onehot + sel.astype(F32)
    before = _dot(lt_ref[...], onehot.astype(BF16)) + run_ref[...]
    e_out = jnp.zeros((rows, LANES), F32)
    p_out = jnp.zeros((rows, LANES), F32)
    r_out = jnp.zeros((rows, LANES), F32)
    for k in range(TOP_K):
        rank = jnp.sum(jnp.where(sels[k], before, 0.0), axis=-1, keepdims=True)
        e_out = jnp.where(out_lane == k, idxs[k], e_out)
        p_out = jnp.where(out_lane == k, exps[k] / denom, p_out)
        r_out = jnp.where(out_lane == k, rank, r_out)
    e_ref[...] = e_out.astype(jnp.int32)
    p_ref[...] = p_out
    r_ref[...] = r_out.astype(jnp.int32)
    run_ref[...] = run_ref[...] + jnp.sum(onehot, axis=0, keepdims=True)
    cnt_ref[...] = jnp.broadcast_to(run_ref[...], cnt_ref.shape).astype(jnp.int32)


def _router_call(lay, l, x, norm_w, mod, w_router, b_router):
    t, d = x.shape
    n_experts = w_router.shape[-1]
    r = np.arange(ROW_BLOCK)
    lower = jnp.asarray((r[None, :] < r[:, None]).astype(np.float32), dtype=BF16)
    wide = pl.BlockSpec((ROW_BLOCK, LANES), lambda i: (i, 0))
    body = functools.partial(_router_body, n_experts=n_experts)
    return pl.pallas_call(
        body,
        grid=(lay.nblk,),
        in_specs=[
            pl.BlockSpec((ROW_BLOCK, d), lambda i: (i, 0)),
            pl.BlockSpec((None, 1, d), lambda i: (l, 0, 0)),
            pl.BlockSpec((None, None, 6, d), lambda i: (l, lay.cond_row(i, ROW_BLOCK), 0, 0)),
            pl.BlockSpec((None, d, n_experts), lambda i: (l, 0, 0)),
            pl.BlockSpec((None, 1, n_experts), lambda i: (l, 0, 0)),
            pl.BlockSpec((ROW_BLOCK, ROW_BLOCK), lambda i: (0, 0)),
        ],
        out_specs=[
            pl.BlockSpec((ROW_BLOCK, d), lambda i: (i, 0)),
            wide, wide, wide,
            pl.BlockSpec((SUBLANES, n_experts), lambda i: (0, 0)),
        ],
        out_shape=[
            jax.ShapeDtypeStruct((t, d), F32),
            jax.ShapeDtypeStruct((t, LANES), jnp.int32),
            jax.ShapeDtypeStruct((t, LANES), F32),
            jax.ShapeDtypeStruct((t, LANES), jnp.int32),
            jax.ShapeDtypeStruct((SUBLANES, n_experts), jnp.int32),
        ],
        scratch_shapes=[pltpu.VMEM((1, n_experts), F32)],
        compiler_params=_params(("arbitrary",), 32),
        name="moe_router",
    )(x, norm_w, mod, w_router, b_router, lower)


def _row_copy(src_hbm, src_row, dst, dst_row, sem):
    return pltpu.make_async_copy(src_hbm.at[pl.ds(src_row, 1)], dst.at[pl.ds(dst_row, 1)], sem)


def _start_row_gather(idx_ref, src_hbm, dst, sem, n_rows):
    def body(r, carry):
        _row_copy(src_hbm, idx_ref[0, 0, r], dst, r, sem).start()
        return carry

    lax.fori_loop(0, n_rows, body, 0)


def _wait_row_gather(src_hbm, dst, sem, n_rows):
    def body(r, carry):
        _row_copy(src_hbm, 0, dst, r, sem).wait()
        return carry

    lax.fori_loop(0, n_rows, body, 0)


def _expert_up_body(be_ref, nu_ref, tok_ref, nxt_ref, h_hbm, w_ref, b_ref, o_ref, buf_ref, sem_ref, *, bm, d_ff):
    i = pl.program_id(0)
    n_used = nu_ref[0]
    slot = lax.rem(i, 2)

    @pl.when(i == 0)
    def _():
        _start_row_gather(tok_ref, h_hbm, buf_ref.at[0], sem_ref.at[0], bm)

    @pl.when(i + 1 < n_used)
    def _():
        _start_row_gather(nxt_ref, h_hbm, buf_ref.at[1 - slot], sem_ref.at[1 - slot], bm)

    @pl.when(i < jnp.maximum(n_used, 1))
    def _():
        _wait_row_gather(h_hbm, buf_ref.at[slot], sem_ref.at[slot], bm)

    @pl.when(i < n_used)
    def _():
        x = buf_ref[slot].astype(BF16)
        gu = _dot(x, w_ref[...]) + b_ref[...]
        glu = jnp.minimum(gu[:, :d_ff], SWIGLU_LIMIT)
        lin = jnp.clip(gu[:, d_ff:], -SWIGLU_LIMIT, SWIGLU_LIMIT)
        o_ref[...] = (glu * _sigmoid(SWIGLU_ALPHA * glu) * (lin + 1.0)).astype(BF16)

    @pl.when(i >= n_used)
    def _():
        o_ref[...] = jnp.zeros_like(o_ref)


def _expert_up_call(l, block_e, n_used, row_tok, h2, w_up, b_up, bm):
    nb = block_e.shape[0]
    d = h2.shape[1]
    d_ff = w_up.shape[-1] // 2
    body = functools.partial(_expert_up_body, bm=bm, d_ff=d_ff)
    grid_spec = pltpu.PrefetchScalarGridSpec(
        num_scalar_prefetch=2,
        grid=(nb,),
        in_specs=[
            pl.BlockSpec((1, 1, bm), lambda i, be, nu: (i, 0, 0), memory_space=pltpu.SMEM),
            pl.BlockSpec((1, 1, bm), lambda i, be, nu: (jnp.minimum(i + 1, nb - 1), 0, 0), memory_space=pltpu.SMEM),
            pl.BlockSpec(memory_space=pl.ANY),
            pl.BlockSpec((None, None, d, 2 * d_ff), lambda i, be, nu: (l, be[i], 0, 0)),
            pl.BlockSpec((None, None, 1, 2 * d_ff), lambda i, be, nu: (l, be[i], 0, 0)),
        ],
        out_specs=pl.BlockSpec((bm, d_ff), lambda i, be, nu: (i, 0)),
        scratch_shapes=[pltpu.VMEM((2, bm, d), F32), pltpu.SemaphoreType.DMA((2,))],
    )
    return pl.pallas_call(
        body,
        grid_spec=grid_spec,
        out_shape=jax.ShapeDtypeStruct((nb * bm, d_ff), BF16),
        compiler_params=_params(("arbitrary",), 56),
        name="moe_expert_up",
    )(block_e, n_used, row_tok, row_tok, h2, w_up, b_up)


def _expert_down_body(be_ref, nu_ref, a_ref, w_ref, b_ref, o_ref):
    @pl.when(pl.program_id(0) < nu_ref[0])
    def _():
        o_ref[...] = _dot(a_ref[...], w_ref[...]) + b_ref[...]

    @pl.when(pl.program_id(0) >= nu_ref[0])
    def _():
        o_ref[...] = jnp.zeros_like(o_ref)


def _expert_down_call(l, block_e, n_used, act, w_down, b_down, bm):
    nb = block_e.shape[0]
    d_ff, d = w_down.shape[-2:]
    grid_spec = pltpu.PrefetchScalarGridSpec(
        num_scalar_prefetch=2,
        grid=(nb,),
        in_specs=[
            pl.BlockSpec((bm, d_ff), lambda i, be, nu: (i, 0)),
            pl.BlockSpec((None, None, d_ff, d), lambda i, be, nu: (l, be[i], 0, 0)),
            pl.BlockSpec((None, None, 1, d), lambda i, be, nu: (l, be[i], 0, 0)),
        ],
        out_specs=pl.BlockSpec((bm, d), lambda i, be, nu: (i, 0)),
    )
    return pl.pallas_call(
        _expert_down_body,
        grid_spec=grid_spec,
        out_shape=jax.ShapeDtypeStruct((nb * bm, d), F32),
        compiler_params=_params(("arbitrary",), 40),
        name="moe_expert_down",
    )(block_e, n_used, act, w_down, b_down)


def _combine_body(pos_ref, nxt_ref, y_hbm, p_ref, x_ref, mod_ref, o_ref, buf_ref, sem_ref, *, nblk):
    i = pl.program_id(0)
    slot = lax.rem(i, 2)

    def start(idx_ref, s):
        def body(r, carry):
            for k in range(TOP_K):
                _row_copy(y_hbm, idx_ref[0, k, r], buf_ref.at[s, k], r, sem_ref.at[s]).start()
            return carry

        lax.fori_loop(0, ROW_BLOCK, body, 0)

    @pl.when(i == 0)
    def _():
        start(pos_ref, 0)

    @pl.when(i + 1 < nblk)
    def _():
        start(nxt_ref, 1 - slot)

    def wait(r, carry):
        for k in range(TOP_K):
            _row_copy(y_hbm, 0, buf_ref.at[slot, k], r, sem_ref.at[slot]).wait()
        return carry

    lax.fori_loop(0, ROW_BLOCK, wait, 0)

    p = p_ref[...]
    acc = p[:, 0:1] * buf_ref[slot, 0]
    for k in range(1, TOP_K):
        acc = acc + p[:, k:k + 1] * buf_ref[slot, k]
    o_ref[...] = x_ref[...] + mod_ref[5:6, :] * acc


def _combine_call(lay, l, pos, y_sorted, top_p, x, mod):
    t, d = x.shape
    nblk = lay.nblk
    body = functools.partial(_combine_body, nblk=nblk)
    return pl.pallas_call(
        body,
        grid=(nblk,),
        in_specs=[
            pl.BlockSpec((1, TOP_K, ROW_BLOCK), lambda i: (i, 0, 0), memory_space=pltpu.SMEM),
            pl.BlockSpec((1, TOP_K, ROW_BLOCK), lambda i: (jnp.minimum(i + 1, nblk - 1), 0, 0),
                         memory_space=pltpu.SMEM),
            pl.BlockSpec(memory_space=pl.ANY),
            pl.BlockSpec((ROW_BLOCK, LANES), lambda i: (i, 0)),
            pl.BlockSpec((ROW_BLOCK, d), lambda i: (i, 0)),
            pl.BlockSpec((None, None, 6, d), lambda i: (l, lay.cond_row(i, ROW_BLOCK), 0, 0)),
        ],
        out_specs=pl.BlockSpec((ROW_BLOCK, d), lambda i: (i, 0)),
        out_shape=jax.ShapeDtypeStruct((t, d), F32),
        scratch_shapes=[pltpu.VMEM((2, TOP_K, ROW_BLOCK, d), F32), pltpu.SemaphoreType.DMA((2,))],
        compiler_params=_params(("arbitrary",), 40),
        name="moe_combine",
    )(pos, pos, y_sorted, top_p, x, mod)


def _moe(lay, l, x, norm_w, mod, w_router, b_router, w_up, b_up, w_down, b_down, bm):
    t = x.shape[0]
    n_experts = w_router.shape[-1]
    h2, top_e, top_p, rank, counts = _router_call(lay, l, x, norm_w, mod, w_router, b_router)
    counts = counts[0]
    padded = (counts + bm - 1) // bm * bm
    pad_end = jnp.cumsum(padded)
    pad_start = pad_end - padded
    e4 = top_e[:, :TOP_K]
    pos = pad_start[e4] + rank[:, :TOP_K]
    nb = t * TOP_K // bm + n_experts
    tok = jnp.broadcast_to(jnp.arange(t, dtype=jnp.int32)[:, None], (t, TOP_K))
    row_tok = jnp.zeros((nb * bm,), jnp.int32).at[pos.reshape(-1)].set(tok.reshape(-1))
    block_e = jnp.minimum(
        jnp.searchsorted(pad_end, jnp.arange(nb, dtype=jnp.int32) * bm, side="right"), n_experts - 1).astype(jnp.int32)
    n_used = (pad_end[-1] // bm).astype(jnp.int32).reshape(1)
    act = _expert_up_call(l, block_e, n_used, row_tok.reshape(nb, 1, bm), h2, w_up, b_up, bm)
    y_sorted = _expert_down_call(l, block_e, n_used, act, w_down, b_down, bm)
    pos3 = pos.reshape(lay.nblk, ROW_BLOCK, TOP_K).transpose(0, 2, 1)
    return _combine_call(lay, l, pos3, y_sorted, top_p, x, mod)


def _final_norm_body(x_ref, w_ref, o_ref):
    x = x_ref[...]
    o_ref[...] = x * lax.rsqrt(jnp.mean(x * x, axis=-1, keepdims=True) + NORM_EPS) * w_ref[...]


def _final_norm_call(x, w):
    t, d = x.shape
    tm = 512
    return pl.pallas_call(
        _final_norm_body,
        grid=(t // tm,),
        in_specs=[pl.BlockSpec((tm, d), lambda i: (i, 0)), pl.BlockSpec((1, d), lambda i: (0, 0))],
        out_specs=pl.BlockSpec((tm, d), lambda i: (i, 0)),
        out_shape=jax.ShapeDtypeStruct((t, d), F32),
        compiler_params=_params(("arbitrary",), 32),
        name="final_norm",
    )(x, w.reshape(1, d))


def _grid_pos(n_tok, d_model):
    rows = n_tok // GRID_W
    r = jnp.repeat(jnp.arange(rows, dtype=F32), GRID_W)
    col = jnp.tile(jnp.arange(GRID_W, dtype=F32), rows)
    quarter = d_model // 4
    omega = 1.0 / (10000.0 ** (jnp.arange(quarter, dtype=F32) / quarter))
    ang_r = r[:, None] * omega
    ang_c = col[:, None] * omega
    return jnp.concatenate([jnp.sin(ang_r), jnp.cos(ang_r), jnp.sin(ang_c), jnp.cos(ang_c)], axis=-1)


def _s5_discretise(lam_re, lam_im, log_step, b_re, b_im):
    step = jnp.exp(log_step)[..., None]
    ang = lam_im * step
    mag = jnp.exp(lam_re * step)
    abar_re = mag * jnp.cos(ang)
    abar_im = mag * jnp.sin(ang)
    num_re = abar_re - 1.0
    den = lam_re * lam_re + lam_im * lam_im
    coef_re = (num_re * lam_re + abar_im * lam_im) / den
    coef_im = (abar_im * lam_re - num_re * lam_im) / den
    bb_re = coef_re[..., None] * b_re[:, None] - coef_im[..., None] * b_im[:, None]
    bb_im = coef_re[..., None] * b_im[:, None] + coef_im[..., None] * b_re[:, None]
    return abar_re, abar_im, bb_re, bb_im


def _block_diag_chunks(m, groups_per_chunk):
    lead = m.shape[:-3]
    g, a, b = m.shape[-3:]
    nch = g // groups_per_chunk
    m = m.reshape(lead + (nch, groups_per_chunk, a, b))
    eye = jnp.eye(groups_per_chunk, dtype=m.dtype)
    out = jnp.einsum("...cgab,gh->...cgahb", m, eye)
    return out.reshape(lead + (nch, groups_per_chunk * a, groups_per_chunk * b))


def kernel(x_prompt, x_sample, state_ssd, state_s5, state_delta, c, c_ctx, norm1, norm2, w_ada, b_ada, w_in,
           ssd_conv_w, ssd_conv_b, ssd_a_log, ssd_dt_bias, ssd_d, ssd_norm, s5_lam_re, s5_lam_im, s5_log_step,
           s5_b_re, s5_b_im, s5_c_re, s5_c_im, s5_d, s5_w_glu, dn_conv_w, dn_a_log, dn_dt_bias, dn_norm,
           w_branch, w_out, w_router, b_router, w_up, b_up, w_down, b_down, final_norm):
    n_ctx, l_ctx, d = x_prompt.shape
    n_smp, l_smp, _ = x_sample.shape
    depth = norm1.shape[0]
    w = d // 2
    lay = _Layout(n_ctx, l_ctx, n_smp, l_smp)
    t = lay.t

    h_ssd = ssd_a_log.shape[-1]
    n_state = state_ssd.shape[-1]
    xbc_w = ssd_conv_w.shape[-1]
    bc_w = xbc_w - w
    gs, ps = s5_lam_re.shape[-2:]
    js = w // gs
    h_dn = dn_a_log.shape[-1]
    n_experts = w_router.shape[-1]
    assert w // h_ssd == CHUNK and n_state == LANES and w // h_dn == LANES and w % 512 == 0
    assert LANES % js == 0 and l_ctx % ROW_BLOCK == 0 and l_smp % ROW_BLOCK == 0
    assert n_ctx % SUBLANES == 0 and n_smp % SUBLANES == 0 and 2 * h_ssd + 4 * h_dn <= LANES
    assert n_smp + 1 <= 2 * SUBLANES

    o_z, o_xbc, o_dt = 0, w, w + xbc_w
    o_u = o_dt + 2 * h_ssd
    o_qkv = o_u + w
    o_g = o_qkv + 3 * w
    o_a = o_g + w
    o_b = o_a + 2 * h_dn
    o_gate = o_b + 2 * h_dn
    seg = lambda a, n: w_in[:, :, a:a + n]
    n_small = 2 * h_ssd + 4 * h_dn
    col_gate, col_z, col_u, col_g, col_qkv = 0, 3 * d, 3 * d + w, 3 * d + 2 * w, 3 * d + 3 * w
    col_small = col_qkv + 3 * w + xbc_w
    n_cols = col_small + LANES
    n_pad = -n_cols % 1024
    w_in_r = jnp.concatenate(
        [seg(o_gate, 3 * d), seg(o_z, w), seg(o_u, w), seg(o_g, w), seg(o_qkv, 3 * w), seg(o_xbc, xbc_w),
         seg(o_dt, 2 * h_ssd), seg(o_a, 2 * h_dn), seg(o_b, 2 * h_dn),
         jnp.zeros((depth, d, LANES - n_small + n_pad), F32)], axis=-1).astype(BF16)
    conv_w = jnp.concatenate([dn_conv_w, ssd_conv_w], axis=-1)
    conv_b = jnp.concatenate([jnp.zeros((depth, 3 * w), F32), ssd_conv_b], axis=-1)[:, None, :]
    conv_width = 3 * w + xbc_w
    small_bias = jnp.concatenate([ssd_dt_bias.reshape(depth, -1), dn_dt_bias.reshape(depth, -1),
                                  jnp.zeros((depth, LANES - 2 * h_ssd - 2 * h_dn), F32)], axis=-1)[:, None, :]

    ssd_aneg = jnp.repeat(-jnp.exp(ssd_a_log), CHUNK, axis=-1)[:, :, None, :]
    ssd_d_exp = jnp.repeat(ssd_d, CHUNK, axis=-1)[:, None, :]
    dn_nega = jnp.repeat(-jnp.exp(dn_a_log), LANES, axis=-1)[:, :, None, :]
    dn_nega_c = jnp.repeat(-jnp.exp(dn_a_log), CHUNK, axis=-1)[:, :, None, :]
    dn_norm_t = jnp.tile(dn_norm, (1, h_dn))[:, None, :]
    e_dt = _expand_matrix(lambda dd, h: dd * h_ssd + h, h_ssd, CHUNK)
    e_g = _expand_matrix(lambda dd, h: 2 * h_ssd + dd * h_dn + h, h_dn, LANES)
    e_gc = _expand_matrix(lambda dd, h: 2 * h_ssd + dd * h_dn + h, h_dn, CHUNK)
    e_beta = _expand_matrix(lambda dd, h: 2 * h_ssd + 2 * h_dn + dd * h_dn + h, h_dn, LANES)
    ssd_consts = _chunk_consts(w)
    dn_consts = _chunk_consts(h_dn * CHUNK)

    gpc = LANES // js
    abar_re, abar_im, bb_re, bb_im = _s5_discretise(s5_lam_re, s5_lam_im, s5_log_step, s5_b_re, s5_b_im)
    n_s5 = gs * ps
    abar_re = abar_re.reshape(depth, 2, 1, n_s5)
    abar_im = abar_im.reshape(depth, 2, 1, n_s5)
    bb_re = _block_diag_chunks(jnp.swapaxes(bb_re, -1, -2), gpc).astype(BF16)
    bb_im = _block_diag_chunks(jnp.swapaxes(bb_im, -1, -2), gpc).astype(BF16)
    cc_re = _block_diag_chunks(jnp.swapaxes(s5_c_re, -1, -2), gpc).astype(BF16)
    cc_im = _block_diag_chunks(jnp.swapaxes(s5_c_im, -1, -2), gpc).astype(BF16)

    ssd_state_t = jnp.swapaxes(state_ssd.reshape(n_smp, depth, 2, w, n_state), -1, -2)
    s5_state = state_s5.reshape(n_smp // SUBLANES, SUBLANES, depth, 2, 2, n_s5).transpose(0, 2, 3, 4, 1, 5)

    w_glu_b = s5_w_glu.astype(BF16)
    w_branch_b = w_branch.astype(BF16)
    w_out_b = w_out.astype(BF16)
    w_up_b = w_up.astype(BF16)
    w_down_b = w_down.astype(BF16)
    b_up4 = b_up[:, :, None, :]
    b_down4 = b_down[:, :, None, :]
    norm1_3 = norm1[:, None, :]
    norm2_3 = norm2[:, None, :]
    b_router3 = b_router[:, None, :]
    ssd_norm3 = ssd_norm[:, None, :]
    s5_d3 = s5_d[:, None, :]

    cond = jnp.zeros((2 * SUBLANES, d), F32).at[0].set(c_ctx).at[1:1 + n_smp].set(c)
    mod = _ada_call(cond, w_ada, b_ada).reshape(depth, 2 * SUBLANES, 6, d)

    xs = x_sample + _grid_pos(l_smp, d)[None]
    x = jnp.concatenate([x_prompt.reshape(lay.t_ctx, d), xs.reshape(n_smp * l_smp, d)], axis=0)

    def to_time_major(a):
        cw = a.shape[-1]
        ctx = a[:lay.t_ctx].reshape(n_ctx // SUBLANES, SUBLANES, l_ctx, cw).transpose(0, 2, 1, 3)
        smp = a[lay.t_ctx:].reshape(n_smp // SUBLANES, SUBLANES, l_smp, cw).transpose(0, 2, 1, 3)
        return jnp.concatenate([ctx.reshape(lay.t_ctx, cw), smp.reshape(t - lay.t_ctx, cw)], axis=0)

    def from_time_major(a):
        cw = a.shape[-1]
        lead = a.shape[:-2]
        ctx = a[..., :lay.t_ctx, :].reshape(lead + (n_ctx // SUBLANES, l_ctx, SUBLANES, cw))
        smp = a[..., lay.t_ctx:, :].reshape(lead + (n_smp // SUBLANES, l_smp, SUBLANES, cw))
        ctx = jnp.swapaxes(ctx, -2, -3).reshape(lead + (lay.t_ctx, cw))
        smp = jnp.swapaxes(smp, -2, -3).reshape(lead + (t - lay.t_ctx, cw))
        return jnp.concatenate([ctx, smp], axis=-2)

    bm = 256
    ssd_out, s5_out, dn_out = [], [], []
    for l in range(depth):
        proj = _in_proj_call(lay, l, x, norm1_3, mod, w_in_r)
        conv = _conv_call(lay, l, proj, conv_w, conv_b, col_qkv, conv_width, 2 * w)
        y_ssd, ssd_fin = _ssd_call(lay, l, conv, proj, col_small, 3 * w, 4 * w, ssd_consts, e_dt, small_bias,
                                   ssd_aneg, ssd_state_t)
        u_tm = to_time_major(proj[:, col_u:col_u + w])
        y_s5_tm, s5_fin = _s5_call(lay, l, u_tm, bb_re, bb_im, cc_re, cc_im, abar_re, abar_im, s5_state)
        y_s5 = from_time_major(y_s5_tm)
        o_dn, dn_fin = _dn_call(lay, l, conv, proj, col_small, dn_consts, e_beta, e_g, e_gc, small_bias,
                                dn_nega, dn_nega_c, state_delta)
        ys = _branch_call(lay, l, y_ssd, conv, 3 * w, proj, col_z, col_u, col_g, y_s5, o_dn,
                          ssd_d_exp, ssd_norm3, s5_d3, dn_norm_t, w_glu_b)
        merged = _merge_call(lay, l, ys, proj, col_gate, w_branch_b)
        x = _out_proj_call(lay, l, merged, w_out_b, x, mod)
        x = _moe(lay, l, x, norm2_3, mod, w_router, b_router3, w_up_b, b_up4, w_down_b, b_down4, bm)

        ssd_out.append(jnp.swapaxes(ssd_fin[:n_ctx], -1, -2).reshape(n_ctx, 2, h_ssd, CHUNK, n_state))
        ng = n_ctx // SUBLANES
        s5_out.append(s5_fin[:ng].transpose(0, 3, 1, 2, 4).reshape(n_ctx, 2, 2, gs, ps))
        dn_out.append(dn_fin[:n_ctx])

    y = _final_norm_call(x, final_norm)
    y_prompt = y[:lay.t_ctx].reshape(n_ctx, l_ctx, d)
    y_sample = y[lay.t_ctx:].reshape(n_smp, l_smp, d)
    return (y_prompt, y_sample, jnp.stack(ssd_out, axis=1), jnp.stack(s5_out, axis=1), jnp.stack(dn_out, axis=1))
```

```python
import functools

import numpy as np
import jax
import jax.numpy as jnp
from jax import lax
from jax.experimental import pallas as pl
from jax.experimental.pallas import tpu as pltpu

F32 = jnp.float32
BF16 = jnp.bfloat16
HIGHEST = lax.Precision.HIGHEST

ROW_BLOCK = 256
CHUNK = 64
LANES = 128
SUBLANES = 8
TOP_K = 4
NORM_EPS = 1e-6
L2_EPS = 1e-6
SWIGLU_ALPHA = 1.702
SWIGLU_LIMIT = 7.0
GRID_W = 64
NEG_BIG = -1e30
MIB = 1024 * 1024


def _params(sem, vmem_mib):
    return pltpu.CompilerParams(dimension_semantics=sem, vmem_limit_bytes=vmem_mib * MIB)


def _dot(a, b, precision=None):
    return jnp.dot(a, b, preferred_element_type=F32, precision=precision)


def _dot_nt(a, b):
    return lax.dot_general(a, b, (((1,), (1,)), ((), ())), preferred_element_type=F32)


def _dot_tn(a, b):
    return lax.dot_general(a, b, (((0,), (0,)), ((), ())), preferred_element_type=F32)


def _sigmoid(x):
    return 1.0 / (1.0 + jnp.exp(-x))


def _silu(x):
    return x * _sigmoid(x)


def _softplus(x):
    return jnp.maximum(x, 0.0) + jnp.log1p(jnp.exp(-jnp.abs(x)))


class _Layout:
    def __init__(self, n_ctx, l_ctx, n_smp, l_smp):
        self.n_ctx, self.l_ctx, self.n_smp, self.l_smp = n_ctx, l_ctx, n_smp, l_smp
        self.t_ctx = n_ctx * l_ctx
        self.t = self.t_ctx + n_smp * l_smp
        self.nblk = self.t // ROW_BLOCK
        self.nblk_ctx = self.t_ctx // ROW_BLOCK
        self.bps_ctx = l_ctx // ROW_BLOCK
        self.bps_smp = l_smp // ROW_BLOCK
        self.nseq = n_ctx + n_smp

    def flags(self, ib):
        is_ctx = ib < self.nblk_ctx
        j = jnp.where(is_ctx, ib, ib - self.nblk_ctx)
        bps = jnp.where(is_ctx, self.bps_ctx, self.bps_smp)
        pos = lax.rem(j, bps)
        seq = jnp.where(is_ctx, lax.div(j, bps), self.n_ctx + lax.div(j, bps))
        return is_ctx, pos == 0, pos == bps - 1, seq

    def smp_seq(self, ib):
        j = jnp.maximum(ib - self.nblk_ctx, 0)
        return lax.div(j, self.bps_smp)

    def cond_row(self, i, rows):
        n_ctx_tiles = self.t_ctx // rows
        return jnp.where(i < n_ctx_tiles, 0, 1 + lax.div(jnp.maximum(i - n_ctx_tiles, 0), self.l_smp // rows))


def _ada_body(c_ref, w_ref, b_ref, o_ref):
    c = c_ref[...]
    o_ref[...] = _dot(_silu(c).astype(BF16), w_ref[...].astype(BF16)) + b_ref[...]


def _ada_call(cond, w_ada, b_ada):
    depth, d, n = w_ada.shape
    rows = cond.shape[0]
    tn = 1024
    return pl.pallas_call(
        _ada_body,
        grid=(depth, n // tn),
        in_specs=[
            pl.BlockSpec((rows, d), lambda l, j: (0, 0)),
            pl.BlockSpec((None, d, tn), lambda l, j: (l, 0, j)),
            pl.BlockSpec((None, 1, tn), lambda l, j: (l, 0, j)),
        ],
        out_specs=pl.BlockSpec((None, rows, tn), lambda l, j: (l, 0, j)),
        out_shape=jax.ShapeDtypeStruct((depth, rows, n), F32),
        compiler_params=_params(("arbitrary", "arbitrary"), 40),
        name="ada_mod",
    )(cond, w_ada, b_ada.reshape(depth, 1, n))


def _modulated_norm(x, nw, shift, scale):
    ms = jnp.mean(x * x, axis=-1, keepdims=True)
    y = x * lax.rsqrt(ms + NORM_EPS) * nw
    return y * (1.0 + scale) + shift


def _in_proj_body(x_ref, nw_ref, mod_ref, w_ref, o_ref, h_ref):
    @pl.when(pl.program_id(1) == 0)
    def _():
        h = _modulated_norm(x_ref[...], nw_ref[...], mod_ref[0:1, :], mod_ref[1:2, :])
        h_ref[...] = h.astype(BF16)

    o_ref[...] = _dot(h_ref[...], w_ref[...])


def _in_proj_call(lay, l, x, norm_w, mod, w_in):
    t, d = x.shape
    n = w_in.shape[-1]
    tm = min(1024, lay.l_smp)
    tn = 1024
    return pl.pallas_call(
        _in_proj_body,
        grid=(t // tm, n // tn),
        in_specs=[
            pl.BlockSpec((tm, d), lambda i, j: (i, 0)),
            pl.BlockSpec((None, 1, d), lambda i, j: (l, 0, 0)),
            pl.BlockSpec((None, None, 6, d), lambda i, j: (l, lay.cond_row(i, tm), 0, 0)),
            pl.BlockSpec((None, d, tn), lambda i, j: (l, 0, j)),
        ],
        out_specs=pl.BlockSpec((tm, tn), lambda i, j: (i, j)),
        out_shape=jax.ShapeDtypeStruct((t, n), F32),
        scratch_shapes=[pltpu.VMEM((tm, d), BF16)],
        compiler_params=_params(("arbitrary", "arbitrary"), 48),
        name="in_proj",
    )(x, norm_w, mod, w_in)


def _conv_body(x_ref, p_ref, n_ref, w_ref, b_ref, o_ref, ext_ref, *, lay, n_qk_blocks, conv_k):
    i = pl.program_id(0)
    c = pl.program_id(1)
    _, first, last, _ = lay.flags(i)
    halo = SUBLANES
    ext_ref[0:halo, :] = jnp.where(first, 0.0, p_ref[...])
    ext_ref[halo:halo + ROW_BLOCK, :] = x_ref[...]
    ext_ref[halo + ROW_BLOCK:2 * halo + ROW_BLOCK, :] = jnp.where(last, 0.0, n_ref[...])
    acc = b_ref[...] + jnp.zeros((ROW_BLOCK, x_ref.shape[1]), F32)
    for k in range(conv_k):
        off = halo - conv_k // 2 + k
        acc = acc + w_ref[k:k + 1, :] * ext_ref[off:off + ROW_BLOCK, :]
    y = _silu(acc)

    @pl.when(c < n_qk_blocks)
    def _():
        for hs in range(y.shape[1] // LANES):
            yh = y[:, hs * LANES:(hs + 1) * LANES]
            ss = jnp.sum(yh * yh, axis=-1, keepdims=True)
            o_ref[:, hs * LANES:(hs + 1) * LANES] = yh * lax.rsqrt(ss + L2_EPS)

    @pl.when(c >= n_qk_blocks)
    def _():
        o_ref[...] = y


def _conv_call(lay, l, proj, conv_w, conv_b, col0, width, qk_width):
    t = proj.shape[0]
    cb = 512
    conv_k = conv_w.shape[1]
    off = col0 // cb
    rb8 = ROW_BLOCK // SUBLANES
    body = functools.partial(_conv_body, lay=lay, n_qk_blocks=qk_width // cb, conv_k=conv_k)
    return pl.pallas_call(
        body,
        grid=(lay.nblk, width // cb),
        in_specs=[
            pl.BlockSpec((ROW_BLOCK, cb), lambda i, c: (i, off + c)),
            pl.BlockSpec((SUBLANES, cb), lambda i, c: (jnp.maximum(i * rb8 - 1, 0), off + c)),
            pl.BlockSpec((SUBLANES, cb), lambda i, c: (jnp.minimum((i + 1) * rb8, t // SUBLANES - 1), off + c)),
            pl.BlockSpec((None, conv_k, cb), lambda i, c: (l, 0, c)),
            pl.BlockSpec((None, 1, cb), lambda i, c: (l, 0, c)),
        ],
        out_specs=pl.BlockSpec((ROW_BLOCK, cb), lambda i, c: (i, c)),
        out_shape=jax.ShapeDtypeStruct((t, width), F32),
        scratch_shapes=[pltpu.VMEM((ROW_BLOCK + 2 * SUBLANES, cb), F32)],
        compiler_params=_params(("arbitrary", "arbitrary"), 24),
        name="dwconv_silu",
    )(proj, proj, proj, conv_w, conv_b)


def _chunk_consts(width):
    r = np.arange(ROW_BLOCK)
    same = (r[:, None] // CHUNK) == (r[None, :] // CHUNK)
    tri = np.stack([same & (r[None, :] <= r[:, None]), same & (r[None, :] >= r[:, None])]).astype(np.float32)
    ones_bd = same.astype(np.float32)
    s = np.arange(width) % CHUNK
    rr = r % CHUNK
    mask = np.stack([rr[:, None] >= s[None, :], rr[:, None] <= s[None, :]]).astype(np.float32)
    eye = (rr[:, None] == s[None, :]).astype(np.float32)
    return jnp.asarray(tri), jnp.asarray(ones_bd), jnp.asarray(mask), jnp.asarray(eye)


def _expand_matrix(col_of_head, n_heads, lanes_per_head):
    e = np.zeros((2, LANES, n_heads * lanes_per_head), np.float32)
    for d in range(2):
        for h in range(n_heads):
            e[d, col_of_head(d, h), h * lanes_per_head:(h + 1) * lanes_per_head] = 1.0
    return jnp.asarray(e)


def _half_masks():
    lane = lax.broadcasted_iota(jnp.int32, (1, LANES), 1)
    m0 = (lane < CHUNK).astype(F32)
    return m0, 1.0 - m0


def _ssd_body(x_ref, bc_ref, sm_ref, e_ref, bias_ref, aneg_ref, tri_ref, ones_ref, mask_ref, eye_ref, s0_ref,
              y_ref, sout_ref, st_ref, cs_ref, ld_ref, xd_ref, *, lay, n_groups, n_state, heads_per_group):
    d = pl.program_id(0)
    i = pl.program_id(1)
    ib = jnp.where(d == 0, i, lay.nblk - 1 - i)
    is_ctx, first, last, _ = lay.flags(ib)
    start = jnp.where(d == 0, first, last)
    end = jnp.where(d == 0, last, first)

    @pl.when(start & is_ctx)
    def _():
        st_ref[...] = jnp.zeros_like(st_ref)

    @pl.when(start & jnp.logical_not(is_ctx))
    def _():
        st_ref[...] = s0_ref[...]

    dt = _dot(_softplus(sm_ref[...] + bias_ref[...]), e_ref[...], HIGHEST)
    a = dt * aneg_ref[...]
    cs = _dot(tri_ref[...], a, HIGHEST)
    cs_cols = _dot(ones_ref[...], eye_ref[...] * cs, HIGHEST)
    cs_ref[...] = cs
    ld_ref[...] = jnp.exp(jnp.where(mask_ref[...] > 0.5, cs - cs_cols, NEG_BIG))
    xd_ref[...] = x_ref[...] * dt

    m0, m1 = _half_masks()
    n_chunks = ROW_BLOCK // CHUNK
    gw = heads_per_group * CHUNK
    for k in range(n_chunks):
        c = jnp.where(d == 0, k, n_chunks - 1 - k)
        rows = pl.ds(pl.multiple_of(c * CHUNK, CHUNK), CHUNK)
        cs_c = cs_ref[rows, :]
        ld_c = ld_ref[rows, :]
        xd_c = xd_ref[rows, :]
        bc_c = bc_ref[rows, :]
        cs_tot = jnp.where(d == 0, cs_c[CHUNK - 1:CHUNK, :], cs_c[0:1, :])
        ecs = jnp.exp(cs_c)
        etot = jnp.exp(cs_tot)
        xdec = xd_c * jnp.exp(cs_tot - cs_c)
        for g in range(n_groups):
            bg = bc_c[:, g * n_state:(g + 1) * n_state].astype(BF16)
            cg = bc_c[:, (n_groups + g) * n_state:(n_groups + g + 1) * n_state].astype(BF16)
            cb2 = _dot_nt(cg, jnp.concatenate([bg, bg], axis=0))
            lo = g * gw
            st_g = st_ref[:, lo:lo + gw]
            y_off = _dot(cg, st_g.astype(BF16)) * ecs[:, lo:lo + gw]
            y_diag = []
            for jj in range(heads_per_group // 2):
                sl = slice(lo + jj * LANES, lo + (jj + 1) * LANES)
                m = (cb2 * ld_c[:, sl]).astype(BF16)
                xj = xd_c[:, sl]
                bd = jnp.concatenate([xj * m0, xj * m1], axis=0).astype(BF16)
                y_diag.append(_dot(m, bd))
            y_ref[rows, lo:lo + gw] = jnp.concatenate(y_diag, axis=1) + y_off
            st_ref[:, lo:lo + gw] = etot[:, lo:lo + gw] * st_g + _dot_tn(bg, xdec[:, lo:lo + gw].astype(BF16))

    @pl.when(end)
    def _():
        sout_ref[...] = st_ref[...]


def _ssd_call(lay, l, conv, proj, small_col, x_col, bc_col, consts, e_dt, bias, aneg, state_t):
    t = conv.shape[0]
    w = aneg.shape[-1]
    n_state = state_t.shape[-2]
    bcw = conv.shape[1] - x_col - w
    n_groups = bcw // (2 * n_state)
    heads = w // CHUNK
    tri, ones_bd, mask, eye = consts
    rev = lambda d, i: jnp.where(d == 0, i, lay.nblk - 1 - i)
    body = functools.partial(_ssd_body, lay=lay, n_groups=n_groups, n_state=n_state,
                             heads_per_group=heads // n_groups)
    return pl.pallas_call(
        body,
        grid=(2, lay.nblk),
        in_specs=[
            pl.BlockSpec((ROW_BLOCK, w), lambda d, i: (rev(d, i), x_col // w)),
            pl.BlockSpec((ROW_BLOCK, bcw), lambda d, i: (rev(d, i), bc_col // bcw)),
            pl.BlockSpec((ROW_BLOCK, LANES), lambda d, i: (rev(d, i), small_col // LANES)),
            pl.BlockSpec((None, LANES, w), lambda d, i: (d, 0, 0)),
            pl.BlockSpec((None, 1, LANES), lambda d, i: (l, 0, 0)),
            pl.BlockSpec((None, None, 1, w), lambda d, i: (l, d, 0, 0)),
            pl.BlockSpec((None, ROW_BLOCK, ROW_BLOCK), lambda d, i: (d, 0, 0)),
            pl.BlockSpec((ROW_BLOCK, ROW_BLOCK), lambda d, i: (0, 0)),
            pl.BlockSpec((None, ROW_BLOCK, w), lambda d, i: (d, 0, 0)),
            pl.BlockSpec((ROW_BLOCK, w), lambda d, i: (0, 0)),
            pl.BlockSpec((None, None, None, n_state, w), lambda d, i: (lay.smp_seq(rev(d, i)), l, d, 0, 0)),
        ],
        out_specs=[
            pl.BlockSpec((None, ROW_BLOCK, w), lambda d, i: (d, rev(d, i), 0)),
            pl.BlockSpec((None, None, n_state, w), lambda d, i: (lay.flags(rev(d, i))[3], d, 0, 0)),
        ],
        out_shape=[
            jax.ShapeDtypeStruct((2, t, w), F32),
            jax.ShapeDtypeStruct((lay.nseq, 2, n_state, w), F32),
        ],
        scratch_shapes=[pltpu.VMEM((n_state, w), F32)] + [pltpu.VMEM((ROW_BLOCK, w), F32)] * 3,
        compiler_params=_params(("arbitrary", "arbitrary"), 40),
        name="ssd_scan",
    )(conv, conv, proj, e_dt, bias, aneg, tri, ones_bd, mask, eye, state_t)


def _s5_body(u_ref, bbr_ref, bbi_ref, cr_ref, ci_ref, ar_ref, ai_ref, h0_ref,
             y_ref, hout_ref, bur_ref, bui_ref, hr_ref, hi_ref, *, nblk, nblk_ctx, bpg_ctx, bpg_smp):
    d = pl.program_id(0)
    i = pl.program_id(1)
    ib = jnp.where(d == 0, i, nblk - 1 - i)
    is_ctx = ib < nblk_ctx
    j = jnp.where(is_ctx, ib, ib - nblk_ctx)
    bpg = jnp.where(is_ctx, bpg_ctx, bpg_smp)
    pos = lax.rem(j, bpg)
    start = jnp.where(d == 0, pos == 0, pos == bpg - 1)
    end = jnp.where(d == 0, pos == bpg - 1, pos == 0)

    @pl.when(start & is_ctx)
    def _():
        hr_ref[...] = jnp.zeros_like(hr_ref)
        hi_ref[...] = jnp.zeros_like(hi_ref)

    @pl.when(start & jnp.logical_not(is_ctx))
    def _():
        hr_ref[...] = h0_ref[0]
        hi_ref[...] = h0_ref[1]

    n_lane_chunks = u_ref.shape[1] // LANES
    sw = bbr_ref.shape[-1]
    for cc in range(n_lane_chunks):
        uc = u_ref[:, cc * LANES:(cc + 1) * LANES].astype(BF16)
        bur_ref[:, cc * sw:(cc + 1) * sw] = _dot(uc, bbr_ref[cc])
        bui_ref[:, cc * sw:(cc + 1) * sw] = _dot(uc, bbi_ref[cc])

    n_steps = ROW_BLOCK // SUBLANES
    n_state = bur_ref.shape[1]
    seg = 512

    def step(s, carry):
        tstep = jnp.where(d == 0, s, n_steps - 1 - s)
        rows = pl.ds(pl.multiple_of(tstep * SUBLANES, SUBLANES), SUBLANES)
        for q in range(n_state // seg):
            sl = slice(q * seg, (q + 1) * seg)
            hr = hr_ref[:, sl]
            hi = hi_ref[:, sl]
            ar = ar_ref[:, sl]
            ai = ai_ref[:, sl]
            nr = ar * hr - ai * hi + bur_ref[rows, sl]
            ni = ar * hi + ai * hr + bui_ref[rows, sl]
            hr_ref[:, sl] = nr
            hi_ref[:, sl] = ni
            bur_ref[rows, sl] = nr
            bui_ref[rows, sl] = ni
        return carry

    lax.fori_loop(0, n_steps, step, 0)

    for cc in range(n_lane_chunks):
        hr = bur_ref[:, cc * sw:(cc + 1) * sw].astype(BF16)
        hi = bui_ref[:, cc * sw:(cc + 1) * sw].astype(BF16)
        y_ref[:, cc * LANES:(cc + 1) * LANES] = _dot(hr, cr_ref[cc]) - _dot(hi, ci_ref[cc])

    @pl.when(end)
    def _():
        hout_ref[0] = hr_ref[...]
        hout_ref[1] = hi_ref[...]


def _s5_call(lay, l, u_tm, bb_re, bb_im, c_re, c_im, abar_re, abar_im, h0):
    t, w = u_tm.shape
    n_state = abar_re.shape[-1]
    nch = w // LANES
    sw = n_state // nch
    ng_ctx = lay.n_ctx // SUBLANES
    ng_smp = lay.n_smp // SUBLANES
    bpg_ctx = lay.l_ctx * SUBLANES // ROW_BLOCK
    bpg_smp = lay.l_smp * SUBLANES // ROW_BLOCK
    rev = lambda d, i: jnp.where(d == 0, i, lay.nblk - 1 - i)

    def group(ib):
        is_ctx = ib < lay.nblk_ctx
        return jnp.where(is_ctx, lax.div(ib, bpg_ctx), ng_ctx + lax.div(jnp.maximum(ib - lay.nblk_ctx, 0), bpg_smp))

    def smp_group(ib):
        return lax.div(jnp.maximum(ib - lay.nblk_ctx, 0), bpg_smp)

    body = functools.partial(_s5_body, nblk=lay.nblk, nblk_ctx=lay.nblk_ctx, bpg_ctx=bpg_ctx, bpg_smp=bpg_smp)
    return pl.pallas_call(
        body,
        grid=(2, lay.nblk),
        in_specs=[
            pl.BlockSpec((ROW_BLOCK, w), lambda d, i: (rev(d, i), 0)),
            pl.BlockSpec((None, None, nch, LANES, sw), lambda d, i: (l, d, 0, 0, 0)),
            pl.BlockSpec((None, None, nch, LANES, sw), lambda d, i: (l, d, 0, 0, 0)),
            pl.BlockSpec((None, nch, sw, LANES), lambda d, i: (l, 0, 0, 0)),
            pl.BlockSpec((None, nch, sw, LANES), lambda d, i: (l, 0, 0, 0)),
            pl.BlockSpec((None, None, 1, n_state), lambda d, i: (l, d, 0, 0)),
            pl.BlockSpec((None, None, 1, n_state), lambda d, i: (l, d, 0, 0)),
            pl.BlockSpec((None, None, None, 2, SUBLANES, n_state),
                         lambda d, i: (smp_group(rev(d, i)), l, d, 0, 0, 0)),
        ],
        out_specs=[
            pl.BlockSpec((None, ROW_BLOCK, w), lambda d, i: (d, rev(d, i), 0)),
            pl.BlockSpec((None, None, 2, SUBLANES, n_state), lambda d, i: (group(rev(d, i)), d, 0, 0, 0)),
        ],
        out_shape=[
            jax.ShapeDtypeStruct((2, t, w), F32),
            jax.ShapeDtypeStruct((ng_ctx + ng_smp, 2, 2, SUBLANES, n_state), F32),
        ],
        scratch_shapes=[pltpu.VMEM((ROW_BLOCK, n_state), F32)] * 2 + [pltpu.VMEM((SUBLANES, n_state), F32)] * 2,
        compiler_params=_params(("arbitrary", "arbitrary"), 40),
        name="s5_scan",
    )(u_tm, bb_re, bb_im, c_re, c_im, abar_re, abar_im, h0)


def _dn_body(q_ref, k_ref, v_ref, sm_ref, eb_ref, eg_ref, egc_ref, bias_ref, nega_ref,
             tri_ref, ones_ref, mask_ref, eye_ref, s0_ref,
             o_ref, sout_ref, s_ref, u_ref, w_ref, at_ref, eg_s, gc_s, *, lay, n_heads):
    d = pl.program_id(0)
    i = pl.program_id(1)
    ib = jnp.where(d == 0, i, lay.nblk - 1 - i)
    is_ctx, first, last, _ = lay.flags(ib)
    start = jnp.where(d == 0, first, last)
    end = jnp.where(d == 0, last, first)
    n_pairs = n_heads // 2
    n_quads = n_heads // 4
    n_chunks = ROW_BLOCK // CHUNK
    dk = LANES
    pw = 2 * LANES
    qw = 4 * LANES
    qc = 4 * CHUNK

    @pl.when(start)
    def _():
        s_ref[...] = jnp.zeros_like(s_ref)

    @pl.when(start & jnp.logical_not(is_ctx))
    def _():
        for p in range(n_pairs):
            s_ref[p, 0:dk, 0:dk] = s0_ref[2 * p]
            s_ref[p, dk:pw, dk:pw] = s0_ref[2 * p + 1]

    scale = dk ** -0.5
    sm = sm_ref[...]
    g_small = _softplus(sm + bias_ref[...]) * nega_ref[...]
    gc_small = _dot(tri_ref[...], g_small, HIGHEST)
    beta = _dot(_sigmoid(sm), eb_ref[...], HIGHEST)
    gc_wide = _dot(gc_small, eg_ref[...], HIGHEST)
    gc_cols = _dot(gc_small, egc_ref[...], HIGHEST)
    gc_t = _dot(ones_ref[...], eye_ref[...] * gc_cols, HIGHEST)
    incl = mask_ref[...]
    dec = jnp.exp(jnp.where(incl > 0.5, gc_cols - gc_t, NEG_BIG))
    strict = incl - eye_ref[...]
    eg = jnp.exp(gc_wide)
    eg_s[...] = eg
    gc_s[...] = gc_wide

    lane_w = lax.broadcasted_iota(jnp.int32, (1, qw), 1)
    head_lanes = [(lax.shift_right_logical(lane_w, 7) == h).astype(F32) for h in range(4)]
    lane_c = lax.broadcasted_iota(jnp.int32, (1, qc), 1)
    head_cols = [(lax.shift_right_logical(lane_c, 6) == h).astype(F32) for h in range(4)]
    ri = lax.broadcasted_iota(jnp.int32, (qc, qc), 0)
    ci = lax.broadcasted_iota(jnp.int32, (qc, qc), 1)
    same = lambda s: lax.shift_right_logical(ri, s) == lax.shift_right_logical(ci, s)
    ident = (ri == ci).astype(F32)
    diag16 = same(4).astype(F32)
    off32 = (same(5) & jnp.logical_not(same(4))).astype(F32)
    off64 = (same(6) & jnp.logical_not(same(5))).astype(F32)
    pair_diag = same(7).astype(F32)

    def stack_heads(x, masks):
        return jnp.concatenate([x * m for m in masks], axis=0)

    for k in range(n_chunks):
        rs = slice(k * CHUNK, (k + 1) * CHUNK)
        for qd in range(n_quads):
            lw = slice(qd * qw, (qd + 1) * qw)
            lq = slice(qd * qc, (qd + 1) * qc)
            kk = k_ref[rs, lw]
            beta_q = beta[rs, lw]
            kb = kk * beta_q
            bdk = stack_heads(kk, head_lanes).astype(BF16)
            kbk = _dot_nt(kb.astype(BF16), bdk)
            dec_q = dec[rs, lq]
            a_bd = stack_heads(kbk * dec_q * strict[rs, lq], head_cols)
            p = -(a_bd * diag16)
            tinv = ident + p
            for _ in range(3):
                pb = p.astype(BF16)
                p = _dot(pb, pb)
                tinv = tinv + _dot(tinv.astype(BF16), p.astype(BF16))
            for off in (off32, off64):
                tb = tinv.astype(BF16)
                tinv = tinv - _dot(_dot(tb, (a_bd * off).astype(BF16)).astype(BF16), tb)
            vb = v_ref[rs, lw] * beta_q
            kbe = kb * eg[rs, lw]
            rhs = jnp.concatenate(
                [jnp.concatenate([vb[:, h * dk:(h + 1) * dk], kbe[:, h * dk:(h + 1) * dk]], axis=1) for h in range(4)],
                axis=0)
            sol = _dot(tinv.astype(BF16), rhs.astype(BF16))
            u_ref[k, qd] = sol[:, :dk]
            w_ref[k, qd] = sol[:, dk:]
            att = _dot_nt((q_ref[rs, lw] * scale).astype(BF16), bdk) * dec_q
            at_ref[k, qd] = stack_heads(att, head_cols).astype(BF16)

    for kk_ in range(n_chunks):
        c = jnp.where(d == 0, kk_, n_chunks - 1 - kk_)
        rows = pl.ds(pl.multiple_of(c * CHUNK, CHUNK), CHUNK)
        gc_c = gc_s[rows, :]
        eg_c = eg_s[rows, :]
        gc_tot = jnp.where(d == 0, gc_c[CHUNK - 1:CHUNK, :], gc_c[0:1, :])
        g_tot = jnp.exp(gc_tot)
        kdec = jnp.exp(gc_tot - gc_c)
        for qd in range(n_quads):
            u_q = u_ref[c, qd]
            w_q = w_ref[c, qd]
            vns, qss, vn_rows = [], [], []
            for pp in range(2):
                p = 2 * qd + pp
                lp = slice(p * pw, (p + 1) * pw)
                ra = slice(pp * LANES, pp * LANES + CHUNK)
                rb = slice(pp * LANES + CHUNK, (pp + 1) * LANES)
                wab = jnp.concatenate([w_q[ra], w_q[rb]], axis=1)
                qdec = q_ref[rows, lp] * scale * eg_c[:, lp]
                res = _dot(jnp.concatenate([wab, qdec], axis=0).astype(BF16), s_ref[p].astype(BF16))
                vn = jnp.concatenate([u_q[ra], u_q[rb]], axis=1) - res[:CHUNK]
                vns.append(vn)
                qss.append(res[CHUNK:])
                vn_rows += [vn[:, :dk], vn[:, dk:]]
            o_q = _dot(at_ref[c, qd], jnp.concatenate(vn_rows, axis=0).astype(BF16))
            for pp in range(2):
                p = 2 * qd + pp
                lp = slice(p * pw, (p + 1) * pw)
                ra = slice(pp * LANES, pp * LANES + CHUNK)
                rb = slice(pp * LANES + CHUNK, (pp + 1) * LANES)
                o_ref[rows, lp] = qss[pp] + jnp.concatenate([o_q[ra], o_q[rb]], axis=1)
                kd = (k_ref[rows, lp] * kdec[:, lp]).astype(BF16)
                s_ref[p] = s_ref[p] * g_tot[:, lp] + _dot_tn(kd, vns[pp].astype(BF16)) * pair_diag

    @pl.when(end)
    def _():
        for p in range(n_pairs):
            sout_ref[2 * p] = s_ref[p, 0:dk, 0:dk]
            sout_ref[2 * p + 1] = s_ref[p, dk:pw, dk:pw]


def _dn_call(lay, l, conv, proj, small_col, consts, e_beta, e_g, e_gc, bias, nega, state):
    t = conv.shape[0]
    w = e_beta.shape[-1]
    n_heads = w // LANES
    assert n_heads % 4 == 0
    wc = n_heads * CHUNK
    tri, ones_bd, mask, eye = consts
    n_chunks = ROW_BLOCK // CHUNK
    rev = lambda d, i: jnp.where(d == 0, i, lay.nblk - 1 - i)
    body = functools.partial(_dn_body, lay=lay, n_heads=n_heads)
    return pl.pallas_call(
        body,
        grid=(2, lay.nblk),
        in_specs=[
            pl.BlockSpec((ROW_BLOCK, w), lambda d, i: (rev(d, i), 0)),
            pl.BlockSpec((ROW_BLOCK, w), lambda d, i: (rev(d, i), 1)),
            pl.BlockSpec((ROW_BLOCK, w), lambda d, i: (rev(d, i), 2)),
            pl.BlockSpec((ROW_BLOCK, LANES), lambda d, i: (rev(d, i), small_col // LANES)),
            pl.BlockSpec((None, LANES, w), lambda d, i: (d, 0, 0)),
            pl.BlockSpec((None, LANES, w), lambda d, i: (d, 0, 0)),
            pl.BlockSpec((None, LANES, wc), lambda d, i: (d, 0, 0)),
            pl.BlockSpec((None, 1, LANES), lambda d, i: (l, 0, 0)),
            pl.BlockSpec((None, 1, LANES), lambda d, i: (l, 0, 0)),
            pl.BlockSpec((None, ROW_BLOCK, ROW_BLOCK), lambda d, i: (d, 0, 0)),
            pl.BlockSpec((ROW_BLOCK, ROW_BLOCK), lambda d, i: (0, 0)),
            pl.BlockSpec((None, ROW_BLOCK, wc), lambda d, i: (d, 0, 0)),
            pl.BlockSpec((ROW_BLOCK, wc), lambda d, i: (0, 0)),
            pl.BlockSpec((None, None, None, n_heads, LANES, LANES),
                         lambda d, i: (lay.smp_seq(rev(d, i)), l, d, 0, 0, 0)),
        ],
        out_specs=[
            pl.BlockSpec((None, ROW_BLOCK, w), lambda d, i: (d, rev(d, i), 0)),
            pl.BlockSpec((None, None, n_heads, LANES, LANES), lambda d, i: (lay.flags(rev(d, i))[3], d, 0, 0, 0)),
        ],
        out_shape=[
            jax.ShapeDtypeStruct((2, t, w), F32),
            jax.ShapeDtypeStruct((lay.nseq, 2, n_heads, LANES, LANES), F32),
        ],
        scratch_shapes=[
            pltpu.VMEM((n_heads // 2, 2 * LANES, 2 * LANES), F32),
            pltpu.VMEM((n_chunks, n_heads // 4, 4 * CHUNK, LANES), F32),
            pltpu.VMEM((n_chunks, n_heads // 4, 4 * CHUNK, LANES), F32),
            pltpu.VMEM((n_chunks, n_heads // 4, 4 * CHUNK, 4 * CHUNK), BF16),
            pltpu.VMEM((ROW_BLOCK, w), F32),
            pltpu.VMEM((ROW_BLOCK, w), F32),
        ],
        compiler_params=_params(("arbitrary", "arbitrary"), 40),
        name="deltanet_scan",
    )(conv, conv, conv, proj, e_beta, e_g, e_gc, bias, nega, tri, ones_bd, mask, eye, state)


def _gelu_tanh(x):
    return 0.5 * x * (1.0 + jnp.tanh(np.sqrt(2.0 / np.pi).astype(np.float32) * (x + 0.044715 * x * x * x)))


def _branch_body(ys_ref, xc_ref, z_ref, y5_ref, u_ref, od_ref, g_ref, dexp_ref, nssd_ref, s5d_ref, ndn_ref,
                 wglu_ref, o_ref):
    w = z_ref.shape[1]
    ya = (ys_ref[0] + ys_ref[1] + dexp_ref[...] * xc_ref[...]) * _silu(z_ref[...])
    ya = ya * lax.rsqrt(jnp.mean(ya * ya, axis=-1, keepdims=True) + NORM_EPS) * nssd_ref[...]
    o_ref[0] = ya.astype(BF16)

    yb = _gelu_tanh(y5_ref[0] + y5_ref[1] + s5d_ref[...] * u_ref[...])
    glu = _dot(yb.astype(BF16), wglu_ref[...])
    o_ref[1] = (glu[:, :w] * _sigmoid(glu[:, w:])).astype(BF16)

    gate = _silu(g_ref[...])
    for h in range(w // LANES):
        sl = slice(h * LANES, (h + 1) * LANES)
        o = od_ref[0, :, sl] + od_ref[1, :, sl]
        o = o * lax.rsqrt(jnp.mean(o * o, axis=-1, keepdims=True) + NORM_EPS) * ndn_ref[:, sl]
        o_ref[2, :, sl] = (o * gate[:, sl]).astype(BF16)


def _branch_call(lay, l, y_ssd, conv, x_col, proj, z_col, u_col, g_col, y_s5, o_dn, d_exp, n_ssd, s5_d, n_dn, w_glu):
    t = conv.shape[0]
    w = d_exp.shape[-1]
    vec = pl.BlockSpec((None, 1, w), lambda i: (l, 0, 0))
    pair = pl.BlockSpec((2, ROW_BLOCK, w), lambda i: (0, i, 0))
    return pl.pallas_call(
        _branch_body,
        grid=(lay.nblk,),
        in_specs=[
            pair,
            pl.BlockSpec((ROW_BLOCK, w), lambda i: (i, x_col // w)),
            pl.BlockSpec((ROW_BLOCK, w), lambda i: (i, z_col // w)),
            pair,
            pl.BlockSpec((ROW_BLOCK, w), lambda i: (i, u_col // w)),
            pair,
            pl.BlockSpec((ROW_BLOCK, w), lambda i: (i, g_col // w)),
            vec, vec, vec, vec,
            pl.BlockSpec((None, w, 2 * w), lambda i: (l, 0, 0)),
        ],
        out_specs=pl.BlockSpec((3, ROW_BLOCK, w), lambda i: (0, i, 0)),
        out_shape=jax.ShapeDtypeStruct((3, t, w), BF16),
        compiler_params=_params(("arbitrary",), 48),
        name="branch_epilogue",
    )(y_ssd, conv, proj, y_s5, proj, o_dn, proj, d_exp, n_ssd, s5_d, n_dn, w_glu)


def _merge_body(ys_ref, g0_ref, g1_ref, g2_ref, wb_ref, o_ref):
    acc = _sigmoid(g0_ref[...]) * _dot(ys_ref[0], wb_ref[0])
    acc = acc + _sigmoid(g1_ref[...]) * _dot(ys_ref[1], wb_ref[1])
    acc = acc + _sigmoid(g2_ref[...]) * _dot(ys_ref[2], wb_ref[2])
    o_ref[...] = acc.astype(BF16)


def _merge_call(lay, l, ys, proj, gate_col, w_branch):
    _, t, w = ys.shape
    d = w_branch.shape[-1]
    g0 = gate_col // d
    gates = [pl.BlockSpec((ROW_BLOCK, d), lambda i, k=k: (i, g0 + k)) for k in range(3)]
    return pl.pallas_call(
        _merge_body,
        grid=(lay.nblk,),
        in_specs=[pl.BlockSpec((3, ROW_BLOCK, w), lambda i: (0, i, 0))] + gates
        + [pl.BlockSpec((None, 3, w, d), lambda i: (l, 0, 0, 0))],
        out_specs=pl.BlockSpec((ROW_BLOCK, d), lambda i: (i, 0)),
        out_shape=jax.ShapeDtypeStruct((t, d), BF16),
        compiler_params=_params(("arbitrary",), 52),
        name="branch_merge",
    )(ys, proj, proj, proj, w_branch)


def _out_proj_body(m_ref, w_ref, x_ref, mod_ref, o_ref):
    o_ref[...] = x_ref[...] + mod_ref[2:3, :] * _dot(m_ref[...], w_ref[...])


def _out_proj_call(lay, l, merged, w_out, x, mod):
    t, d = x.shape
    tm = min(512, lay.l_smp)
    return pl.pallas_call(
        _out_proj_body,
        grid=(t // tm,),
        in_specs=[
            pl.BlockSpec((tm, d), lambda i: (i, 0)),
            pl.BlockSpec((None, d, d), lambda i: (l, 0, 0)),
            pl.BlockSpec((tm, d), lambda i: (i, 0)),
            pl.BlockSpec((None, None, 6, d), lambda i: (l, lay.cond_row(i, tm), 0, 0)),
        ],
        out_specs=pl.BlockSpec((tm, d), lambda i: (i, 0)),
        out_shape=jax.ShapeDtypeStruct((t, d), F32),
        compiler_params=_params(("arbitrary",), 48),
        name="out_proj",
    )(merged, w_out, x, mod)


def _router_body(x_ref, nw_ref, mod_ref, wr_ref, br_ref, lt_ref,
                 h_ref, e_ref, p_ref, r_ref, cnt_ref, run_ref, *, n_experts):
    i = pl.program_id(0)

    @pl.when(i == 0)
    def _():
        run_ref[...] = jnp.zeros_like(run_ref)

    h = _modulated_norm(x_ref[...], nw_ref[...], mod_ref[3:4, :], mod_ref[4:5, :])
    h_ref[...] = h
    logits = _dot(h, wr_ref[...], HIGHEST) + br_ref[...]
    rows = logits.shape[0]
    lane = lax.broadcasted_iota(jnp.int32, (rows, n_experts), 1).astype(F32)
    out_lane = lax.broadcasted_iota(jnp.int32, (rows, LANES), 1)
    cur = logits
    sels, tops, idxs = [], [], []
    for _ in range(TOP_K):
        m = jnp.max(cur, axis=-1, keepdims=True)
        idx = jnp.min(jnp.where(cur == m, lane, float(n_experts)), axis=-1, keepdims=True)
        sel = lane == idx
        sels.append(sel)
        tops.append(m)
        idxs.append(idx)
        cur = jnp.where(sel, -jnp.inf, cur)
    exps = [jnp.exp(m - tops[0]) for m in tops]
    denom = exps[0] + exps[1] + exps[2] + exps[3]
    onehot = jnp.zeros((rows, n_experts), F32)
    for sel in sels:
        onehot = onehot + sel.astype(F32)
    before = _dot(lt_ref[...], onehot.astype(BF16)) + run_ref[...]
    e_out = jnp.zeros((rows, LANES), F32)
    p_out = jnp.zeros((rows, LANES), F32)
    r_out = jnp.zeros((rows, LANES), F32)
    for k in range(TOP_K):
        rank = jnp.sum(jnp.where(sels[k], before, 0.0), axis=-1, keepdims=True)
        e_out = jnp.where(out_lane == k, idxs[k], e_out)
        p_out = jnp.where(out_lane == k, exps[k] / denom, p_out)
        r_out = jnp.where(out_lane == k, rank, r_out)
    e_ref[...] = e_out.astype(jnp.int32)
    p_ref[...] = p_out
    r_ref[...] = r_out.astype(jnp.int32)
    run_ref[...] = run_ref[...] + jnp.sum(onehot, axis=0, keepdims=True)
    cnt_ref[...] = jnp.broadcast_to(run_ref[...], cnt_ref.shape).astype(jnp.int32)


def _router_call(lay, l, x, norm_w, mod, w_router, b_router):
    t, d = x.shape
    n_experts = w_router.shape[-1]
    r = np.arange(ROW_BLOCK)
    lower = jnp.asarray((r[None, :] < r[:, None]).astype(np.float32), dtype=BF16)
    wide = pl.BlockSpec((ROW_BLOCK, LANES), lambda i: (i, 0))
    body = functools.partial(_router_body, n_experts=n_experts)
    return pl.pallas_call(
        body,
        grid=(lay.nblk,),
        in_specs=[
            pl.BlockSpec((ROW_BLOCK, d), lambda i: (i, 0)),
            pl.BlockSpec((None, 1, d), lambda i: (l, 0, 0)),
            pl.BlockSpec((None, None, 6, d), lambda i: (l, lay.cond_row(i, ROW_BLOCK), 0, 0)),
            pl.BlockSpec((None, d, n_experts), lambda i: (l, 0, 0)),
            pl.BlockSpec((None, 1, n_experts), lambda i: (l, 0, 0)),
            pl.BlockSpec((ROW_BLOCK, ROW_BLOCK), lambda i: (0, 0)),
        ],
        out_specs=[
            pl.BlockSpec((ROW_BLOCK, d), lambda i: (i, 0)),
            wide, wide, wide,
            pl.BlockSpec((SUBLANES, n_experts), lambda i: (0, 0)),
        ],
        out_shape=[
            jax.ShapeDtypeStruct((t, d), F32),
            jax.ShapeDtypeStruct((t, LANES), jnp.int32),
            jax.ShapeDtypeStruct((t, LANES), F32),
            jax.ShapeDtypeStruct((t, LANES), jnp.int32),
            jax.ShapeDtypeStruct((SUBLANES, n_experts), jnp.int32),
        ],
        scratch_shapes=[pltpu.VMEM((1, n_experts), F32)],
        compiler_params=_params(("arbitrary",), 32),
        name="moe_router",
    )(x, norm_w, mod, w_router, b_router, lower)


def _row_copy(src_hbm, src_row, dst, dst_row, sem):
    return pltpu.make_async_copy(src_hbm.at[pl.ds(src_row, 1)], dst.at[pl.ds(dst_row, 1)], sem)


def _start_row_gather(idx_ref, src_hbm, dst, sem, n_rows):
    def body(r, carry):
        _row_copy(src_hbm, idx_ref[0, 0, r], dst, r, sem).start()
        return carry

    lax.fori_loop(0, n_rows, body, 0)


def _wait_row_gather(src_hbm, dst, sem, n_rows):
    def body(r, carry):
        _row_copy(src_hbm, 0, dst, r, sem).wait()
        return carry

    lax.fori_loop(0, n_rows, body, 0)


def _expert_up_body(be_ref, nu_ref, tok_ref, nxt_ref, h_hbm, w_ref, b_ref, o_ref, buf_ref, sem_ref, *, bm, d_ff):
    i = pl.program_id(0)
    n_used = nu_ref[0]
    slot = lax.rem(i, 2)

    @pl.when(i == 0)
    def _():
        _start_row_gather(tok_ref, h_hbm, buf_ref.at[0], sem_ref.at[0], bm)

    @pl.when(i + 1 < n_used)
    def _():
        _start_row_gather(nxt_ref, h_hbm, buf_ref.at[1 - slot], sem_ref.at[1 - slot], bm)

    @pl.when(i < jnp.maximum(n_used, 1))
    def _():
        _wait_row_gather(h_hbm, buf_ref.at[slot], sem_ref.at[slot], bm)

    @pl.when(i < n_used)
    def _():
        x = buf_ref[slot].astype(BF16)
        gu = _dot(x, w_ref[...]) + b_ref[...]
        glu = jnp.minimum(gu[:, :d_ff], SWIGLU_LIMIT)
        lin = jnp.clip(gu[:, d_ff:], -SWIGLU_LIMIT, SWIGLU_LIMIT)
        o_ref[...] = (glu * _sigmoid(SWIGLU_ALPHA * glu) * (lin + 1.0)).astype(BF16)

    @pl.when(i >= n_used)
    def _():
        o_ref[...] = jnp.zeros_like(o_ref)


def _expert_up_call(l, block_e, n_used, row_tok, h2, w_up, b_up, bm):
    nb = block_e.shape[0]
    d = h2.shape[1]
    d_ff = w_up.shape[-1] // 2
    body = functools.partial(_expert_up_body, bm=bm, d_ff=d_ff)
    grid_spec = pltpu.PrefetchScalarGridSpec(
        num_scalar_prefetch=2,
        grid=(nb,),
        in_specs=[
            pl.BlockSpec((1, 1, bm), lambda i, be, nu: (i, 0, 0), memory_space=pltpu.SMEM),
            pl.BlockSpec((1, 1, bm), lambda i, be, nu: (jnp.minimum(i + 1, nb - 1), 0, 0), memory_space=pltpu.SMEM),
            pl.BlockSpec(memory_space=pl.ANY),
            pl.BlockSpec((None, None, d, 2 * d_ff), lambda i, be, nu: (l, be[i], 0, 0)),
            pl.BlockSpec((None, None, 1, 2 * d_ff), lambda i, be, nu: (l, be[i], 0, 0)),
        ],
        out_specs=pl.BlockSpec((bm, d_ff), lambda i, be, nu: (i, 0)),
        scratch_shapes=[pltpu.VMEM((2, bm, d), F32), pltpu.SemaphoreType.DMA((2,))],
    )
    return pl.pallas_call(
        body,
        grid_spec=grid_spec,
        out_shape=jax.ShapeDtypeStruct((nb * bm, d_ff), BF16),
        compiler_params=_params(("arbitrary",), 56),
        name="moe_expert_up",
    )(block_e, n_used, row_tok, row_tok, h2, w_up, b_up)


def _expert_down_body(be_ref, nu_ref, a_ref, w_ref, b_ref, o_ref):
    @pl.when(pl.program_id(0) < nu_ref[0])
    def _():
        o_ref[...] = _dot(a_ref[...], w_ref[...]) + b_ref[...]

    @pl.when(pl.program_id(0) >= nu_ref[0])
    def _():
        o_ref[...] = jnp.zeros_like(o_ref)


def _expert_down_call(l, block_e, n_used, act, w_down, b_down, bm):
    nb = block_e.shape[0]
    d_ff, d = w_down.shape[-2:]
    grid_spec = pltpu.PrefetchScalarGridSpec(
        num_scalar_prefetch=2,
        grid=(nb,),
        in_specs=[
            pl.BlockSpec((bm, d_ff), lambda i, be, nu: (i, 0)),
            pl.BlockSpec((None, None, d_ff, d), lambda i, be, nu: (l, be[i], 0, 0)),
            pl.BlockSpec((None, None, 1, d), lambda i, be, nu: (l, be[i], 0, 0)),
        ],
        out_specs=pl.BlockSpec((bm, d), lambda i, be, nu: (i, 0)),
    )
    return pl.pallas_call(
        _expert_down_body,
        grid_spec=grid_spec,
        out_shape=jax.ShapeDtypeStruct((nb * bm, d), F32),
        compiler_params=_params(("arbitrary",), 40),
        name="moe_expert_down",
    )(block_e, n_used, act, w_down, b_down)


def _combine_body(pos_ref, nxt_ref, y_hbm, p_ref, x_ref, mod_ref, o_ref, buf_ref, sem_ref, *, nblk):
    i = pl.program_id(0)
    slot = lax.rem(i, 2)

    def start(idx_ref, s):
        def body(r, carry):
            for k in range(TOP_K):
                _row_copy(y_hbm, idx_ref[0, k, r], buf_ref.at[s, k], r, sem_ref.at[s]).start()
            return carry

        lax.fori_loop(0, ROW_BLOCK, body, 0)

    @pl.when(i == 0)
    def _():
        start(pos_ref, 0)

    @pl.when(i + 1 < nblk)
    def _():
        start(nxt_ref, 1 - slot)

    def wait(r, carry):
        for k in range(TOP_K):
            _row_copy(y_hbm, 0, buf_ref.at[slot, k], r, sem_ref.at[slot]).wait()
        return carry

    lax.fori_loop(0, ROW_BLOCK, wait, 0)

    p = p_ref[...]
    acc = p[:, 0:1] * buf_ref[slot, 0]
    for k in range(1, TOP_K):
        acc = acc + p[:, k:k + 1] * buf_ref[slot, k]
    o_ref[...] = x_ref[...] + mod_ref[5:6, :] * acc


def _combine_call(lay, l, pos, y_sorted, top_p, x, mod):
    t, d = x.shape
    nblk = lay.nblk
    body = functools.partial(_combine_body, nblk=nblk)
    return pl.pallas_call(
        body,
        grid=(nblk,),
        in_specs=[
            pl.BlockSpec((1, TOP_K, ROW_BLOCK), lambda i: (i, 0, 0), memory_space=pltpu.SMEM),
            pl.BlockSpec((1, TOP_K, ROW_BLOCK), lambda i: (jnp.minimum(i + 1, nblk - 1), 0, 0),
                         memory_space=pltpu.SMEM),
            pl.BlockSpec(memory_space=pl.ANY),
            pl.BlockSpec((ROW_BLOCK, LANES), lambda i: (i, 0)),
            pl.BlockSpec((ROW_BLOCK, d), lambda i: (i, 0)),
            pl.BlockSpec((None, None, 6, d), lambda i: (l, lay.cond_row(i, ROW_BLOCK), 0, 0)),
        ],
        out_specs=pl.BlockSpec((ROW_BLOCK, d), lambda i: (i, 0)),
        out_shape=jax.ShapeDtypeStruct((t, d), F32),
        scratch_shapes=[pltpu.VMEM((2, TOP_K, ROW_BLOCK, d), F32), pltpu.SemaphoreType.DMA((2,))],
        compiler_params=_params(("arbitrary",), 40),
        name="moe_combine",
    )(pos, pos, y_sorted, top_p, x, mod)


def _moe(lay, l, x, norm_w, mod, w_router, b_router, w_up, b_up, w_down, b_down, bm):
    t = x.shape[0]
    n_experts = w_router.shape[-1]
    h2, top_e, top_p, rank, counts = _router_call(lay, l, x, norm_w, mod, w_router, b_router)
    counts = counts[0]
    padded = (counts + bm - 1) // bm * bm
    pad_end = jnp.cumsum(padded)
    pad_start = pad_end - padded
    e4 = top_e[:, :TOP_K]
    pos = pad_start[e4] + rank[:, :TOP_K]
    nb = t * TOP_K // bm + n_experts
    tok = jnp.broadcast_to(jnp.arange(t, dtype=jnp.int32)[:, None], (t, TOP_K))
    row_tok = jnp.zeros((nb * bm,), jnp.int32).at[pos.reshape(-1)].set(tok.reshape(-1))
    block_e = jnp.minimum(
        jnp.searchsorted(pad_end, jnp.arange(nb, dtype=jnp.int32) * bm, side="right"), n_experts - 1).astype(jnp.int32)
    n_used = (pad_end[-1] // bm).astype(jnp.int32).reshape(1)
    act = _expert_up_call(l, block_e, n_used, row_tok.reshape(nb, 1, bm), h2, w_up, b_up, bm)
    y_sorted = _expert_down_call(l, block_e, n_used, act, w_down, b_down, bm)
    pos3 = pos.reshape(lay.nblk, ROW_BLOCK, TOP_K).transpose(0, 2, 1)
    return _combine_call(lay, l, pos3, y_sorted, top_p, x, mod)


def _final_norm_body(x_ref, w_ref, o_ref):
    x = x_ref[...]
    o_ref[...] = x * lax.rsqrt(jnp.mean(x * x, axis=-1, keepdims=True) + NORM_EPS) * w_ref[...]


def _final_norm_call(x, w):
    t, d = x.shape
    tm = 512
    return pl.pallas_call(
        _final_norm_body,
        grid=(t // tm,),
        in_specs=[pl.BlockSpec((tm, d), lambda i: (i, 0)), pl.BlockSpec((1, d), lambda i: (0, 0))],
        out_specs=pl.BlockSpec((tm, d), lambda i: (i, 0)),
        out_shape=jax.ShapeDtypeStruct((t, d), F32),
        compiler_params=_params(("arbitrary",), 32),
        name="final_norm",
    )(x, w.reshape(1, d))


def _grid_pos(n_tok, d_model):
    rows = n_tok // GRID_W
    r = jnp.repeat(jnp.arange(rows, dtype=F32), GRID_W)
    col = jnp.tile(jnp.arange(GRID_W, dtype=F32), rows)
    quarter = d_model // 4
    omega = 1.0 / (10000.0 ** (jnp.arange(quarter, dtype=F32) / quarter))
    ang_r = r[:, None] * omega
    ang_c = col[:, None] * omega
    return jnp.concatenate([jnp.sin(ang_r), jnp.cos(ang_r), jnp.sin(ang_c), jnp.cos(ang_c)], axis=-1)


def _s5_discretise(lam_re, lam_im, log_step, b_re, b_im):
    step = jnp.exp(log_step)[..., None]
    ang = lam_im * step
    mag = jnp.exp(lam_re * step)
    abar_re = mag * jnp.cos(ang)
    abar_im = mag * jnp.sin(ang)
    num_re = abar_re - 1.0
    den = lam_re * lam_re + lam_im * lam_im
    coef_re = (num_re * lam_re + abar_im * lam_im) / den
    coef_im = (abar_im * lam_re - num_re * lam_im) / den
    bb_re = coef_re[..., None] * b_re[:, None] - coef_im[..., None] * b_im[:, None]
    bb_im = coef_re[..., None] * b_im[:, None] + coef_im[..., None] * b_re[:, None]
    return abar_re, abar_im, bb_re, bb_im


def _block_diag_chunks(m, groups_per_chunk):
    lead = m.shape[:-3]
    g, a, b = m.shape[-3:]
    nch = g // groups_per_chunk
    m = m.reshape(lead + (nch, groups_per_chunk, a, b))
    eye = jnp.eye(groups_per_chunk, dtype=m.dtype)
    out = jnp.einsum("...cgab,gh->...cgahb", m, eye)
    return out.reshape(lead + (nch, groups_per_chunk * a, groups_per_chunk * b))


def kernel(x_prompt, x_sample, state_ssd, state_s5, state_delta, c, c_ctx, norm1, norm2, w_ada, b_ada, w_in,
           ssd_conv_w, ssd_conv_b, ssd_a_log, ssd_dt_bias, ssd_d, ssd_norm, s5_lam_re, s5_lam_im, s5_log_step,
           s5_b_re, s5_b_im, s5_c_re, s5_c_im, s5_d, s5_w_glu, dn_conv_w, dn_a_log, dn_dt_bias, dn_norm,
           w_branch, w_out, w_router, b_router, w_up, b_up, w_down, b_down, final_norm):
    n_ctx, l_ctx, d = x_prompt.shape
    n_smp, l_smp, _ = x_sample.shape
    depth = norm1.shape[0]
    w = d // 2
    lay = _Layout(n_ctx, l_ctx, n_smp, l_smp)
    t = lay.t

    h_ssd = ssd_a_log.shape[-1]
    n_state = state_ssd.shape[-1]
    xbc_w = ssd_conv_w.shape[-1]
    bc_w = xbc_w - w
    gs, ps = s5_lam_re.shape[-2:]
    js = w // gs
    h_dn = dn_a_log.shape[-1]
    n_experts = w_router.shape[-1]
    assert w // h_ssd == CHUNK and n_state == LANES and w // h_dn == LANES and w % 512 == 0
    assert LANES % js == 0 and l_ctx % ROW_BLOCK == 0 and l_smp % ROW_BLOCK == 0
    assert n_ctx % SUBLANES == 0 and n_smp % SUBLANES == 0 and 2 * h_ssd + 4 * h_dn <= LANES
    assert n_smp + 1 <= 2 * SUBLANES

    o_z, o_xbc, o_dt = 0, w, w + xbc_w
    o_u = o_dt + 2 * h_ssd
    o_qkv = o_u + w
    o_g = o_qkv + 3 * w
    o_a = o_g + w
    o_b = o_a + 2 * h_dn
    o_gate = o_b + 2 * h_dn
    seg = lambda a, n: w_in[:, :, a:a + n]
    n_small = 2 * h_ssd + 4 * h_dn
    col_gate, col_z, col_u, col_g, col_qkv = 0, 3 * d, 3 * d + w, 3 * d + 2 * w, 3 * d + 3 * w
    col_small = col_qkv + 3 * w + xbc_w
    n_cols = col_small + LANES
    n_pad = -n_cols % 1024
    w_in_r = jnp.concatenate(
        [seg(o_gate, 3 * d), seg(o_z, w), seg(o_u, w), seg(o_g, w), seg(o_qkv, 3 * w), seg(o_xbc, xbc_w),
         seg(o_dt, 2 * h_ssd), seg(o_a, 2 * h_dn), seg(o_b, 2 * h_dn),
         jnp.zeros((depth, d, LANES - n_small + n_pad), F32)], axis=-1).astype(BF16)
    conv_w = jnp.concatenate([dn_conv_w, ssd_conv_w], axis=-1)
    conv_b = jnp.concatenate([jnp.zeros((depth, 3 * w), F32), ssd_conv_b], axis=-1)[:, None, :]
    conv_width = 3 * w + xbc_w
    small_bias = jnp.concatenate([ssd_dt_bias.reshape(depth, -1), dn_dt_bias.reshape(depth, -1),
                                  jnp.zeros((depth, LANES - 2 * h_ssd - 2 * h_dn), F32)], axis=-1)[:, None, :]

    ssd_aneg = jnp.repeat(-jnp.exp(ssd_a_log), CHUNK, axis=-1)[:, :, None, :]
    ssd_d_exp = jnp.repeat(ssd_d, CHUNK, axis=-1)[:, None, :]
    dn_nega = jnp.concatenate([jnp.zeros((depth, 2 * h_ssd), F32), -jnp.exp(dn_a_log).reshape(depth, -1),
                               jnp.zeros((depth, LANES - 2 * h_ssd - 2 * h_dn), F32)], axis=-1)[:, None, :]
    dn_norm_t = jnp.tile(dn_norm, (1, h_dn))[:, None, :]
    e_dt = _expand_matrix(lambda dd, h: dd * h_ssd + h, h_ssd, CHUNK)
    e_g = _expand_matrix(lambda dd, h: 2 * h_ssd + dd * h_dn + h, h_dn, LANES)
    e_gc = _expand_matrix(lambda dd, h: 2 * h_ssd + dd * h_dn + h, h_dn, CHUNK)
    e_beta = _expand_matrix(lambda dd, h: 2 * h_ssd + 2 * h_dn + dd * h_dn + h, h_dn, LANES)
    ssd_consts = _chunk_consts(w)
    dn_consts = _chunk_consts(h_dn * CHUNK)

    gpc = LANES // js
    abar_re, abar_im, bb_re, bb_im = _s5_discretise(s5_lam_re, s5_lam_im, s5_log_step, s5_b_re, s5_b_im)
    n_s5 = gs * ps
    abar_re = abar_re.reshape(depth, 2, 1, n_s5)
    abar_im = abar_im.reshape(depth, 2, 1, n_s5)
    bb_re = _block_diag_chunks(jnp.swapaxes(bb_re, -1, -2), gpc).astype(BF16)
    bb_im = _block_diag_chunks(jnp.swapaxes(bb_im, -1, -2), gpc).astype(BF16)
    cc_re = _block_diag_chunks(jnp.swapaxes(s5_c_re, -1, -2), gpc).astype(BF16)
    cc_im = _block_diag_chunks(jnp.swapaxes(s5_c_im, -1, -2), gpc).astype(BF16)

    ssd_state_t = jnp.swapaxes(state_ssd.reshape(n_smp, depth, 2, w, n_state), -1, -2)
    s5_state = state_s5.reshape(n_smp // SUBLANES, SUBLANES, depth, 2, 2, n_s5).transpose(0, 2, 3, 4, 1, 5)

    w_glu_b = s5_w_glu.astype(BF16)
    w_branch_b = w_branch.astype(BF16)
    w_out_b = w_out.astype(BF16)
    w_up_b = w_up.astype(BF16)
    w_down_b = w_down.astype(BF16)
    b_up4 = b_up[:, :, None, :]
    b_down4 = b_down[:, :, None, :]
    norm1_3 = norm1[:, None, :]
    norm2_3 = norm2[:, None, :]
    b_router3 = b_router[:, None, :]
    ssd_norm3 = ssd_norm[:, None, :]
    s5_d3 = s5_d[:, None, :]

    cond = jnp.zeros((2 * SUBLANES, d), F32).at[0].set(c_ctx).at[1:1 + n_smp].set(c)
    mod = _ada_call(cond, w_ada, b_ada).reshape(depth, 2 * SUBLANES, 6, d)

    xs = x_sample + _grid_pos(l_smp, d)[None]
    x = jnp.concatenate([x_prompt.reshape(lay.t_ctx, d), xs.reshape(n_smp * l_smp, d)], axis=0)

    def to_time_major(a):
        cw = a.shape[-1]
        ctx = a[:lay.t_ctx].reshape(n_ctx // SUBLANES, SUBLANES, l_ctx, cw).transpose(0, 2, 1, 3)
        smp = a[lay.t_ctx:].reshape(n_smp // SUBLANES, SUBLANES, l_smp, cw).transpose(0, 2, 1, 3)
        return jnp.concatenate([ctx.reshape(lay.t_ctx, cw), smp.reshape(t - lay.t_ctx, cw)], axis=0)

    def from_time_major(a):
        cw = a.shape[-1]
        lead = a.shape[:-2]
        ctx = a[..., :lay.t_ctx, :].reshape(lead + (n_ctx // SUBLANES, l_ctx, SUBLANES, cw))
        smp = a[..., lay.t_ctx:, :].reshape(lead + (n_smp // SUBLANES, l_smp, SUBLANES, cw))
        ctx = jnp.swapaxes(ctx, -2, -3).reshape(lead + (lay.t_ctx, cw))
        smp = jnp.swapaxes(smp, -2, -3).reshape(lead + (t - lay.t_ctx, cw))
        return jnp.concatenate([ctx, smp], axis=-2)

    bm = 256
    ssd_out, s5_out, dn_out = [], [], []
    for l in range(depth):
        proj = _in_proj_call(lay, l, x, norm1_3, mod, w_in_r)
        conv = _conv_call(lay, l, proj, conv_w, conv_b, col_qkv, conv_width, 2 * w)
        y_ssd, ssd_fin = _ssd_call(lay, l, conv, proj, col_small, 3 * w, 4 * w, ssd_consts, e_dt, small_bias,
                                   ssd_aneg, ssd_state_t)
        u_tm = to_time_major(proj[:, col_u:col_u + w])
        y_s5_tm, s5_fin = _s5_call(lay, l, u_tm, bb_re, bb_im, cc_re, cc_im, abar_re, abar_im, s5_state)
        y_s5 = from_time_major(y_s5_tm)
        o_dn, dn_fin = _dn_call(lay, l, conv, proj, col_small, dn_consts, e_beta, e_g, e_gc, small_bias,
                                dn_nega, state_delta)
        ys = _branch_call(lay, l, y_ssd, conv, 3 * w, proj, col_z, col_u, col_g, y_s5, o_dn,
                          ssd_d_exp, ssd_norm3, s5_d3, dn_norm_t, w_glu_b)
        merged = _merge_call(lay, l, ys, proj, col_gate, w_branch_b)
        x = _out_proj_call(lay, l, merged, w_out_b, x, mod)
        x = _moe(lay, l, x, norm2_3, mod, w_router, b_router3, w_up_b, b_up4, w_down_b, b_down4, bm)

        ssd_out.append(jnp.swapaxes(ssd_fin[:n_ctx], -1, -2).reshape(n_ctx, 2, h_ssd, CHUNK, n_state))
        ng = n_ctx // SUBLANES
        s5_out.append(s5_fin[:ng].transpose(0, 3, 1, 2, 4).reshape(n_ctx, 2, 2, gs, ps))
        dn_out.append(dn_fin[:n_ctx])

    y = _final_norm_call(x, final_norm)
    y_prompt = y[:lay.t_ctx].reshape(n_ctx, l_ctx, d)
    y_sample = y[lay.t_ctx:].reshape(n_smp, l_smp, d)
    return (y_prompt, y_sample, jnp.stack(ssd_out, axis=1), jnp.stack(s5_out, axis=1), jnp.stack(dn_out, axis=1))
```

```python
import functools

import numpy as np
import jax
import jax.numpy as jnp
from jax import lax
from jax.experimental import pallas as pl
from jax.experimental.pallas import tpu as pltpu

F32 = jnp.float32
BF16 = jnp.bfloat16
HIGHEST = lax.Precision.HIGHEST

ROW_BLOCK = 256
CHUNK = 64
LANES = 128
SUBLANES = 8
TOP_K = 4
NORM_EPS = 1e-6
L2_EPS = 1e-6
SWIGLU_ALPHA = 1.702
SWIGLU_LIMIT = 7.0
GRID_W = 64
NEG_BIG = -1e30
MIB = 1024 * 1024


def _params(sem, vmem_mib):
    return pltpu.CompilerParams(dimension_semantics=sem, vmem_limit_bytes=vmem_mib * MIB)


def _dot(a, b, precision=None):
    return jnp.dot(a, b, preferred_element_type=F32, precision=precision)


def _dot_nt(a, b):
    return lax.dot_general(a, b, (((1,), (1,)), ((), ())), preferred_element_type=F32)


def _dot_tn(a, b):
    return lax.dot_general(a, b, (((0,), (0,)), ((), ())), preferred_element_type=F32)


def _split_bf16(x, parts):
    out = []
    for n in range(parts):
        piece = x.astype(BF16)
        out.append(piece)
        if n + 1 < parts:
            x = x - piece.astype(F32)
    return out


def _dot_01_left(m01, x, parts=3):
    pieces = _split_bf16(x, parts)
    acc = _dot(m01, pieces[0])
    for piece in pieces[1:]:
        acc = acc + _dot(m01, piece)
    return acc


def _dot_01_right(x, m01, parts=3):
    pieces = _split_bf16(x, parts)
    acc = _dot(pieces[0], m01)
    for piece in pieces[1:]:
        acc = acc + _dot(piece, m01)
    return acc


def _sigmoid(x):
    return 1.0 / (1.0 + jnp.exp(-x))


def _silu(x):
    return x * _sigmoid(x)


def _softplus(x):
    return jnp.maximum(x, 0.0) + jnp.log1p(jnp.exp(-jnp.abs(x)))


class _Layout:
    def __init__(self, n_ctx, l_ctx, n_smp, l_smp):
        self.n_ctx, self.l_ctx, self.n_smp, self.l_smp = n_ctx, l_ctx, n_smp, l_smp
        self.t_ctx = n_ctx * l_ctx
        self.t = self.t_ctx + n_smp * l_smp
        self.nblk = self.t // ROW_BLOCK
        self.nblk_ctx = self.t_ctx // ROW_BLOCK
        self.bps_ctx = l_ctx // ROW_BLOCK
        self.bps_smp = l_smp // ROW_BLOCK
        self.nseq = n_ctx + n_smp

    def flags(self, ib):
        is_ctx = ib < self.nblk_ctx
        j = jnp.where(is_ctx, ib, ib - self.nblk_ctx)
        bps = jnp.where(is_ctx, self.bps_ctx, self.bps_smp)
        pos = lax.rem(j, bps)
        seq = jnp.where(is_ctx, lax.div(j, bps), self.n_ctx + lax.div(j, bps))
        return is_ctx, pos == 0, pos == bps - 1, seq

    def smp_seq(self, ib):
        j = jnp.maximum(ib - self.nblk_ctx, 0)
        return lax.div(j, self.bps_smp)

    def cond_row(self, i, rows):
        n_ctx_tiles = self.t_ctx // rows
        return jnp.where(i < n_ctx_tiles, 0, 1 + lax.div(jnp.maximum(i - n_ctx_tiles, 0), self.l_smp // rows))


def _ada_body(c_ref, w_ref, b_ref, o_ref):
    c = c_ref[...]
    o_ref[...] = _dot(_silu(c).astype(BF16), w_ref[...].astype(BF16)) + b_ref[...]


def _ada_call(cond, w_ada, b_ada):
    depth, d, n = w_ada.shape
    rows = cond.shape[0]
    tn = 1024
    return pl.pallas_call(
        _ada_body,
        grid=(depth, n // tn),
        in_specs=[
            pl.BlockSpec((rows, d), lambda l, j: (0, 0)),
            pl.BlockSpec((None, d, tn), lambda l, j: (l, 0, j)),
            pl.BlockSpec((None, 1, tn), lambda l, j: (l, 0, j)),
        ],
        out_specs=pl.BlockSpec((None, rows, tn), lambda l, j: (l, 0, j)),
        out_shape=jax.ShapeDtypeStruct((depth, rows, n), F32),
        compiler_params=_params(("arbitrary", "arbitrary"), 40),
        name="ada_mod",
    )(cond, w_ada, b_ada.reshape(depth, 1, n))


def _modulated_norm(x, nw, shift, scale):
    ms = jnp.mean(x * x, axis=-1, keepdims=True)
    y = x * lax.rsqrt(ms + NORM_EPS) * nw
    return y * (1.0 + scale) + shift


def _in_proj_body(x_ref, nw_ref, mod_ref, w_ref, o_ref, h_ref):
    @pl.when(pl.program_id(1) == 0)
    def _():
        h = _modulated_norm(x_ref[...], nw_ref[...], mod_ref[0:1, :], mod_ref[1:2, :])
        h_ref[...] = h.astype(BF16)

    o_ref[...] = _dot(h_ref[...], w_ref[...])


def _in_proj_call(lay, l, x, norm_w, mod, w_in):
    t, d = x.shape
    n = w_in.shape[-1]
    tm = min(1024, lay.l_smp)
    tn = 1024
    return pl.pallas_call(
        _in_proj_body,
        grid=(t // tm, n // tn),
        in_specs=[
            pl.BlockSpec((tm, d), lambda i, j: (i, 0)),
            pl.BlockSpec((None, 1, d), lambda i, j: (l, 0, 0)),
            pl.BlockSpec((None, None, 6, d), lambda i, j: (l, lay.cond_row(i, tm), 0, 0)),
            pl.BlockSpec((None, d, tn), lambda i, j: (l, 0, j)),
        ],
        out_specs=pl.BlockSpec((tm, tn), lambda i, j: (i, j)),
        out_shape=jax.ShapeDtypeStruct((t, n), F32),
        scratch_shapes=[pltpu.VMEM((tm, d), BF16)],
        compiler_params=_params(("arbitrary", "arbitrary"), 48),
        name="in_proj",
    )(x, norm_w, mod, w_in)


def _conv_body(x_ref, p_ref, n_ref, w_ref, b_ref, o_ref, ext_ref, *, lay, n_qk_blocks, conv_k):
    i = pl.program_id(0)
    c = pl.program_id(1)
    _, first, last, _ = lay.flags(i)
    halo = SUBLANES
    ext_ref[0:halo, :] = jnp.where(first, 0.0, p_ref[...])
    ext_ref[halo:halo + ROW_BLOCK, :] = x_ref[...]
    ext_ref[halo + ROW_BLOCK:2 * halo + ROW_BLOCK, :] = jnp.where(last, 0.0, n_ref[...])
    acc = b_ref[...] + jnp.zeros((ROW_BLOCK, x_ref.shape[1]), F32)
    for k in range(conv_k):
        off = halo - conv_k // 2 + k
        acc = acc + w_ref[k:k + 1, :] * ext_ref[off:off + ROW_BLOCK, :]
    y = _silu(acc)

    @pl.when(c < n_qk_blocks)
    def _():
        for hs in range(y.shape[1] // LANES):
            yh = y[:, hs * LANES:(hs + 1) * LANES]
            ss = jnp.sum(yh * yh, axis=-1, keepdims=True)
            o_ref[:, hs * LANES:(hs + 1) * LANES] = yh * lax.rsqrt(ss + L2_EPS)

    @pl.when(c >= n_qk_blocks)
    def _():
        o_ref[...] = y


def _conv_call(lay, l, proj, conv_w, conv_b, col0, width, qk_width):
    t = proj.shape[0]
    cb = 512
    conv_k = conv_w.shape[1]
    off = col0 // cb
    rb8 = ROW_BLOCK // SUBLANES
    body = functools.partial(_conv_body, lay=lay, n_qk_blocks=qk_width // cb, conv_k=conv_k)
    return pl.pallas_call(
        body,
        grid=(lay.nblk, width // cb),
        in_specs=[
            pl.BlockSpec((ROW_BLOCK, cb), lambda i, c: (i, off + c)),
            pl.BlockSpec((SUBLANES, cb), lambda i, c: (jnp.maximum(i * rb8 - 1, 0), off + c)),
            pl.BlockSpec((SUBLANES, cb), lambda i, c: (jnp.minimum((i + 1) * rb8, t // SUBLANES - 1), off + c)),
            pl.BlockSpec((None, conv_k, cb), lambda i, c: (l, 0, c)),
            pl.BlockSpec((None, 1, cb), lambda i, c: (l, 0, c)),
        ],
        out_specs=pl.BlockSpec((ROW_BLOCK, cb), lambda i, c: (i, c)),
        out_shape=jax.ShapeDtypeStruct((t, width), F32),
        scratch_shapes=[pltpu.VMEM((ROW_BLOCK + 2 * SUBLANES, cb), F32)],
        compiler_params=_params(("arbitrary", "arbitrary"), 24),
        name="dwconv_silu",
    )(proj, proj, proj, conv_w, conv_b)


def _chunk_consts(width):
    r = np.arange(ROW_BLOCK)
    same = (r[:, None] // CHUNK) == (r[None, :] // CHUNK)
    tri = np.stack([same & (r[None, :] <= r[:, None]), same & (r[None, :] >= r[:, None])]).astype(np.float32)
    ones_bd = same.astype(np.float32)
    s = np.arange(width) % CHUNK
    rr = r % CHUNK
    mask = np.stack([rr[:, None] >= s[None, :], rr[:, None] <= s[None, :]]).astype(np.float32)
    eye = (rr[:, None] == s[None, :]).astype(np.float32)
    return jnp.asarray(tri, dtype=BF16), jnp.asarray(ones_bd, dtype=BF16), jnp.asarray(mask), jnp.asarray(eye)


def _expand_matrix(col_of_head, n_heads, lanes_per_head):
    e = np.zeros((2, LANES, n_heads * lanes_per_head), np.float32)
    for d in range(2):
        for h in range(n_heads):
            e[d, col_of_head(d, h), h * lanes_per_head:(h + 1) * lanes_per_head] = 1.0
    return jnp.asarray(e, dtype=BF16)


def _half_masks():
    lane = lax.broadcasted_iota(jnp.int32, (1, LANES), 1)
    m0 = (lane < CHUNK).astype(F32)
    return m0, 1.0 - m0


def _ssd_body(x_ref, bc_ref, sm_ref, e_ref, bias_ref, aneg_ref, tri_ref, ones_ref, mask_ref, eye_ref, s0_ref,
              y_ref, sout_ref, st_ref, cs_ref, ld_ref, xd_ref, *, lay, n_groups, n_state, heads_per_group):
    d = pl.program_id(0)
    i = pl.program_id(1)
    ib = jnp.where(d == 0, i, lay.nblk - 1 - i)
    is_ctx, first, last, _ = lay.flags(ib)
    start = jnp.where(d == 0, first, last)
    end = jnp.where(d == 0, last, first)

    @pl.when(start & is_ctx)
    def _():
        st_ref[...] = jnp.zeros_like(st_ref)

    @pl.when(start & jnp.logical_not(is_ctx))
    def _():
        st_ref[...] = s0_ref[...]

    dt = _dot_01_right(_softplus(sm_ref[...] + bias_ref[...]), e_ref[...])
    a = dt * aneg_ref[...]
    cs = _dot_01_left(tri_ref[...], a)
    cs_cols = _dot_01_left(ones_ref[...], eye_ref[...] * cs)
    cs_ref[...] = cs
    ld_ref[...] = jnp.exp(jnp.where(mask_ref[...] > 0.5, cs - cs_cols, NEG_BIG))
    xd_ref[...] = x_ref[...] * dt

    m0, m1 = _half_masks()
    n_chunks = ROW_BLOCK // CHUNK
    gw = heads_per_group * CHUNK
    for k in range(n_chunks):
        c = jnp.where(d == 0, k, n_chunks - 1 - k)
        rows = pl.ds(pl.multiple_of(c * CHUNK, CHUNK), CHUNK)
        cs_c = cs_ref[rows, :]
        ld_c = ld_ref[rows, :]
        xd_c = xd_ref[rows, :]
        bc_c = bc_ref[rows, :]
        cs_tot = jnp.where(d == 0, cs_c[CHUNK - 1:CHUNK, :], cs_c[0:1, :])
        ecs = jnp.exp(cs_c)
        etot = jnp.exp(cs_tot)
        xdec = xd_c * jnp.exp(cs_tot - cs_c)
        for g in range(n_groups):
            bg = bc_c[:, g * n_state:(g + 1) * n_state].astype(BF16)
            cg = bc_c[:, (n_groups + g) * n_state:(n_groups + g + 1) * n_state].astype(BF16)
            cb2 = _dot_nt(cg, jnp.concatenate([bg, bg], axis=0))
            lo = g * gw
            st_g = st_ref[:, lo:lo + gw]
            y_off = _dot(cg, st_g.astype(BF16)) * ecs[:, lo:lo + gw]
            y_diag = []
            for jj in range(heads_per_group // 2):
                sl = slice(lo + jj * LANES, lo + (jj + 1) * LANES)
                m = (cb2 * ld_c[:, sl]).astype(BF16)
                xj = xd_c[:, sl]
                bd = jnp.concatenate([xj * m0, xj * m1], axis=0).astype(BF16)
                y_diag.append(_dot(m, bd))
            y_ref[rows, lo:lo + gw] = jnp.concatenate(y_diag, axis=1) + y_off
            st_ref[:, lo:lo + gw] = etot[:, lo:lo + gw] * st_g + _dot_tn(bg, xdec[:, lo:lo + gw].astype(BF16))

    @pl.when(end)
    def _():
        sout_ref[...] = st_ref[...]


def _ssd_call(lay, l, conv, proj, small_col, x_col, bc_col, consts, e_dt, bias, aneg, state_t):
    t = conv.shape[0]
    w = aneg.shape[-1]
    n_state = state_t.shape[-2]
    bcw = conv.shape[1] - x_col - w
    n_groups = bcw // (2 * n_state)
    heads = w // CHUNK
    tri, ones_bd, mask, eye = consts
    rev = lambda d, i: jnp.where(d == 0, i, lay.nblk - 1 - i)
    body = functools.partial(_ssd_body, lay=lay, n_groups=n_groups, n_state=n_state,
                             heads_per_group=heads // n_groups)
    return pl.pallas_call(
        body,
        grid=(2, lay.nblk),
        in_specs=[
            pl.BlockSpec((ROW_BLOCK, w), lambda d, i: (rev(d, i), x_col // w)),
            pl.BlockSpec((ROW_BLOCK, bcw), lambda d, i: (rev(d, i), bc_col // bcw)),
            pl.BlockSpec((ROW_BLOCK, LANES), lambda d, i: (rev(d, i), small_col // LANES)),
            pl.BlockSpec((None, LANES, w), lambda d, i: (d, 0, 0)),
            pl.BlockSpec((None, 1, LANES), lambda d, i: (l, 0, 0)),
            pl.BlockSpec((None, None, 1, w), lambda d, i: (l, d, 0, 0)),
            pl.BlockSpec((None, ROW_BLOCK, ROW_BLOCK), lambda d, i: (d, 0, 0)),
            pl.BlockSpec((ROW_BLOCK, ROW_BLOCK), lambda d, i: (0, 0)),
            pl.BlockSpec((None, ROW_BLOCK, w), lambda d, i: (d, 0, 0)),
            pl.BlockSpec((ROW_BLOCK, w), lambda d, i: (0, 0)),
            pl.BlockSpec((None, None, None, n_state, w), lambda d, i: (lay.smp_seq(rev(d, i)), l, d, 0, 0)),
        ],
        out_specs=[
            pl.BlockSpec((None, ROW_BLOCK, w), lambda d, i: (d, rev(d, i), 0)),
            pl.BlockSpec((None, None, n_state, w), lambda d, i: (lay.flags(rev(d, i))[3], d, 0, 0)),
        ],
        out_shape=[
            jax.ShapeDtypeStruct((2, t, w), F32),
            jax.ShapeDtypeStruct((lay.nseq, 2, n_state, w), F32),
        ],
        scratch_shapes=[pltpu.VMEM((n_state, w), F32)] + [pltpu.VMEM((ROW_BLOCK, w), F32)] * 3,
        compiler_params=_params(("arbitrary", "arbitrary"), 40),
        name="ssd_scan",
    )(conv, conv, proj, e_dt, bias, aneg, tri, ones_bd, mask, eye, state_t)


def _s5_body(u_ref, bbr_ref, bbi_ref, cr_ref, ci_ref, ar_ref, ai_ref, h0_ref,
             y_ref, hout_ref, bur_ref, bui_ref, hr_ref, hi_ref, *, nblk, nblk_ctx, bpg_ctx, bpg_smp):
    d = pl.program_id(0)
    i = pl.program_id(1)
    ib = jnp.where(d == 0, i, nblk - 1 - i)
    is_ctx = ib < nblk_ctx
    j = jnp.where(is_ctx, ib, ib - nblk_ctx)
    bpg = jnp.where(is_ctx, bpg_ctx, bpg_smp)
    pos = lax.rem(j, bpg)
    start = jnp.where(d == 0, pos == 0, pos == bpg - 1)
    end = jnp.where(d == 0, pos == bpg - 1, pos == 0)

    @pl.when(start & is_ctx)
    def _():
        hr_ref[...] = jnp.zeros_like(hr_ref)
        hi_ref[...] = jnp.zeros_like(hi_ref)

    @pl.when(start & jnp.logical_not(is_ctx))
    def _():
        hr_ref[...] = h0_ref[0]
        hi_ref[...] = h0_ref[1]

    n_lane_chunks = u_ref.shape[1] // LANES
    sw = bbr_ref.shape[-1]
    for cc in range(n_lane_chunks):
        uc = u_ref[:, cc * LANES:(cc + 1) * LANES].astype(BF16)
        bur_ref[:, cc * sw:(cc + 1) * sw] = _dot(uc, bbr_ref[cc])
        bui_ref[:, cc * sw:(cc + 1) * sw] = _dot(uc, bbi_ref[cc])

    n_steps = ROW_BLOCK // SUBLANES
    n_state = bur_ref.shape[1]
    seg = 512

    def step(s, carry):
        tstep = jnp.where(d == 0, s, n_steps - 1 - s)
        rows = pl.ds(pl.multiple_of(tstep * SUBLANES, SUBLANES), SUBLANES)
        for q in range(n_state // seg):
            sl = slice(q * seg, (q + 1) * seg)
            hr = hr_ref[:, sl]
            hi = hi_ref[:, sl]
            ar = ar_ref[:, sl]
            ai = ai_ref[:, sl]
            nr = ar * hr - ai * hi + bur_ref[rows, sl]
            ni = ar * hi + ai * hr + bui_ref[rows, sl]
            hr_ref[:, sl] = nr
            hi_ref[:, sl] = ni
            bur_ref[rows, sl] = nr
            bui_ref[rows, sl] = ni
        return carry

    lax.fori_loop(0, n_steps, step, 0)

    for cc in range(n_lane_chunks):
        hr = bur_ref[:, cc * sw:(cc + 1) * sw].astype(BF16)
        hi = bui_ref[:, cc * sw:(cc + 1) * sw].astype(BF16)
        y_ref[:, cc * LANES:(cc + 1) * LANES] = _dot(hr, cr_ref[cc]) - _dot(hi, ci_ref[cc])

    @pl.when(end)
    def _():
        hout_ref[0] = hr_ref[...]
        hout_ref[1] = hi_ref[...]


def _s5_call(lay, l, u_tm, bb_re, bb_im, c_re, c_im, abar_re, abar_im, h0):
    t, w = u_tm.shape
    n_state = abar_re.shape[-1]
    nch = w // LANES
    sw = n_state // nch
    ng_ctx = lay.n_ctx // SUBLANES
    ng_smp = lay.n_smp // SUBLANES
    bpg_ctx = lay.l_ctx * SUBLANES // ROW_BLOCK
    bpg_smp = lay.l_smp * SUBLANES // ROW_BLOCK
    rev = lambda d, i: jnp.where(d == 0, i, lay.nblk - 1 - i)

    def group(ib):
        is_ctx = ib < lay.nblk_ctx
        return jnp.where(is_ctx, lax.div(ib, bpg_ctx), ng_ctx + lax.div(jnp.maximum(ib - lay.nblk_ctx, 0), bpg_smp))

    def smp_group(ib):
        return lax.div(jnp.maximum(ib - lay.nblk_ctx, 0), bpg_smp)

    body = functools.partial(_s5_body, nblk=lay.nblk, nblk_ctx=lay.nblk_ctx, bpg_ctx=bpg_ctx, bpg_smp=bpg_smp)
    return pl.pallas_call(
        body,
        grid=(2, lay.nblk),
        in_specs=[
            pl.BlockSpec((ROW_BLOCK, w), lambda d, i: (rev(d, i), 0)),
            pl.BlockSpec((None, None, nch, LANES, sw), lambda d, i: (l, d, 0, 0, 0)),
            pl.BlockSpec((None, None, nch, LANES, sw), lambda d, i: (l, d, 0, 0, 0)),
            pl.BlockSpec((None, nch, sw, LANES), lambda d, i: (l, 0, 0, 0)),
            pl.BlockSpec((None, nch, sw, LANES), lambda d, i: (l, 0, 0, 0)),
            pl.BlockSpec((None, None, 1, n_state), lambda d, i: (l, d, 0, 0)),
            pl.BlockSpec((None, None, 1, n_state), lambda d, i: (l, d, 0, 0)),
            pl.BlockSpec((None, None, None, 2, SUBLANES, n_state),
                         lambda d, i: (smp_group(rev(d, i)), l, d, 0, 0, 0)),
        ],
        out_specs=[
            pl.BlockSpec((None, ROW_BLOCK, w), lambda d, i: (d, rev(d, i), 0)),
            pl.BlockSpec((None, None, 2, SUBLANES, n_state), lambda d, i: (group(rev(d, i)), d, 0, 0, 0)),
        ],
        out_shape=[
            jax.ShapeDtypeStruct((2, t, w), F32),
            jax.ShapeDtypeStruct((ng_ctx + ng_smp, 2, 2, SUBLANES, n_state), F32),
        ],
        scratch_shapes=[pltpu.VMEM((ROW_BLOCK, n_state), F32)] * 2 + [pltpu.VMEM((SUBLANES, n_state), F32)] * 2,
        compiler_params=_params(("arbitrary", "arbitrary"), 40),
        name="s5_scan",
    )(u_tm, bb_re, bb_im, c_re, c_im, abar_re, abar_im, h0)


def _dn_body(q_ref, k_ref, v_ref, sm_ref, eb_ref, eg_ref, egc_ref, bias_ref, nega_ref,
             tri_ref, ones_ref, mask_ref, eye_ref, s0_ref,
             o_ref, sout_ref, s_ref, u_ref, w_ref, at_ref, eg_s, gc_s, *, lay, n_heads):
    d = pl.program_id(0)
    i = pl.program_id(1)
    ib = jnp.where(d == 0, i, lay.nblk - 1 - i)
    is_ctx, first, last, _ = lay.flags(ib)
    start = jnp.where(d == 0, first, last)
    end = jnp.where(d == 0, last, first)
    n_pairs = n_heads // 2
    n_quads = n_heads // 4
    n_chunks = ROW_BLOCK // CHUNK
    dk = LANES
    pw = 2 * LANES
    qw = 4 * LANES
    qc = 4 * CHUNK

    @pl.when(start)
    def _():
        s_ref[...] = jnp.zeros_like(s_ref)

    @pl.when(start & jnp.logical_not(is_ctx))
    def _():
        for p in range(n_pairs):
            s_ref[p, 0:dk, 0:dk] = s0_ref[2 * p]
            s_ref[p, dk:pw, dk:pw] = s0_ref[2 * p + 1]

    scale = dk ** -0.5
    sm = sm_ref[...]
    g_small = _softplus(sm + bias_ref[...]) * nega_ref[...]
    gc_small = _dot_01_left(tri_ref[...], g_small)
    beta = _dot_01_right(_sigmoid(sm), eb_ref[...], parts=2)
    gc_wide = _dot_01_right(gc_small, eg_ref[...])
    gc_cols = _dot_01_right(gc_small, egc_ref[...])
    gc_t = _dot_01_left(ones_ref[...], eye_ref[...] * gc_cols)
    incl = mask_ref[...]
    dec = jnp.exp(jnp.where(incl > 0.5, gc_cols - gc_t, NEG_BIG))
    strict = incl - eye_ref[...]
    eg = jnp.exp(gc_wide)
    eg_s[...] = eg
    gc_s[...] = gc_wide

    lane_w = lax.broadcasted_iota(jnp.int32, (1, qw), 1)
    head_lanes = [(lax.shift_right_logical(lane_w, 7) == h).astype(F32) for h in range(4)]
    lane_c = lax.broadcasted_iota(jnp.int32, (1, qc), 1)
    head_cols = [(lax.shift_right_logical(lane_c, 6) == h).astype(F32) for h in range(4)]
    ri = lax.broadcasted_iota(jnp.int32, (qc, qc), 0)
    ci = lax.broadcasted_iota(jnp.int32, (qc, qc), 1)
    same = lambda s: lax.shift_right_logical(ri, s) == lax.shift_right_logical(ci, s)
    ident = (ri == ci).astype(F32)
    diag16 = same(4).astype(F32)
    off32 = (same(5) & jnp.logical_not(same(4))).astype(F32)
    off64 = (same(6) & jnp.logical_not(same(5))).astype(F32)
    pair_diag = same(7).astype(F32)

    def stack_heads(x, masks):
        return jnp.concatenate([x * m for m in masks], axis=0)

    for k in range(n_chunks):
        rs = slice(k * CHUNK, (k + 1) * CHUNK)
        for qd in range(n_quads):
            lw = slice(qd * qw, (qd + 1) * qw)
            lq = slice(qd * qc, (qd + 1) * qc)
            kk = k_ref[rs, lw]
            beta_q = beta[rs, lw]
            kb = kk * beta_q
            bdk = stack_heads(kk, head_lanes).astype(BF16)
            kbk = _dot_nt(kb.astype(BF16), bdk)
            dec_q = dec[rs, lq]
            a_bd = stack_heads(kbk * dec_q * strict[rs, lq], head_cols)
            p = -(a_bd * diag16)
            tinv = ident + p
            for _ in range(3):
                pb = p.astype(BF16)
                p = _dot(pb, pb)
                tinv = tinv + _dot(tinv.astype(BF16), p.astype(BF16))
            for off in (off32, off64):
                tb = tinv.astype(BF16)
                tinv = tinv - _dot(_dot(tb, (a_bd * off).astype(BF16)).astype(BF16), tb)
            vb = v_ref[rs, lw] * beta_q
            kbe = kb * eg[rs, lw]
            rhs = jnp.concatenate(
                [jnp.concatenate([vb[:, h * dk:(h + 1) * dk], kbe[:, h * dk:(h + 1) * dk]], axis=1) for h in range(4)],
                axis=0)
            sol = _dot(tinv.astype(BF16), rhs.astype(BF16))
            u_ref[k, qd] = sol[:, :dk]
            w_ref[k, qd] = sol[:, dk:]
            att = _dot_nt((q_ref[rs, lw] * scale).astype(BF16), bdk) * dec_q
            at_ref[k, qd] = stack_heads(att, head_cols).astype(BF16)

    for kk_ in range(n_chunks):
        c = jnp.where(d == 0, kk_, n_chunks - 1 - kk_)
        rows = pl.ds(pl.multiple_of(c * CHUNK, CHUNK), CHUNK)
        gc_c = gc_s[rows, :]
        eg_c = eg_s[rows, :]
        gc_tot = jnp.where(d == 0, gc_c[CHUNK - 1:CHUNK, :], gc_c[0:1, :])
        g_tot = jnp.exp(gc_tot)
        kdec = jnp.exp(gc_tot - gc_c)
        for qd in range(n_quads):
            u_q = u_ref[c, qd]
            w_q = w_ref[c, qd]
            vns, qss, vn_rows = [], [], []
            for pp in range(2):
                p = 2 * qd + pp
                lp = slice(p * pw, (p + 1) * pw)
                ra = slice(pp * LANES, pp * LANES + CHUNK)
                rb = slice(pp * LANES + CHUNK, (pp + 1) * LANES)
                wab = jnp.concatenate([w_q[ra], w_q[rb]], axis=1)
                qdec = q_ref[rows, lp] * scale * eg_c[:, lp]
                res = _dot(jnp.concatenate([wab, qdec], axis=0).astype(BF16), s_ref[p].astype(BF16))
                vn = jnp.concatenate([u_q[ra], u_q[rb]], axis=1) - res[:CHUNK]
                vns.append(vn)
                qss.append(res[CHUNK:])
                vn_rows += [vn[:, :dk], vn[:, dk:]]
            o_q = _dot(at_ref[c, qd], jnp.concatenate(vn_rows, axis=0).astype(BF16))
            for pp in range(2):
                p = 2 * qd + pp
                lp = slice(p * pw, (p + 1) * pw)
                ra = slice(pp * LANES, pp * LANES + CHUNK)
                rb = slice(pp * LANES + CHUNK, (pp + 1) * LANES)
                o_ref[rows, lp] = qss[pp] + jnp.concatenate([o_q[ra], o_q[rb]], axis=1)
                kd = (k_ref[rows, lp] * kdec[:, lp]).astype(BF16)
                s_ref[p] = s_ref[p] * g_tot[:, lp] + _dot_tn(kd, vns[pp].astype(BF16)) * pair_diag

    @pl.when(end)
    def _():
        for p in range(n_pairs):
            sout_ref[2 * p] = s_ref[p, 0:dk, 0:dk]
            sout_ref[2 * p + 1] = s_ref[p, dk:pw, dk:pw]


def _dn_call(lay, l, conv, proj, small_col, consts, e_beta, e_g, e_gc, bias, nega, state):
    t = conv.shape[0]
    w = e_beta.shape[-1]
    n_heads = w // LANES
    assert n_heads % 4 == 0
    wc = n_heads * CHUNK
    tri, ones_bd, mask, eye = consts
    n_chunks = ROW_BLOCK // CHUNK
    rev = lambda d, i: jnp.where(d == 0, i, lay.nblk - 1 - i)
    body = functools.partial(_dn_body, lay=lay, n_heads=n_heads)
    return pl.pallas_call(
        body,
        grid=(2, lay.nblk),
        in_specs=[
            pl.BlockSpec((ROW_BLOCK, w), lambda d, i: (rev(d, i), 0)),
            pl.BlockSpec((ROW_BLOCK, w), lambda d, i: (rev(d, i), 1)),
            pl.BlockSpec((ROW_BLOCK, w), lambda d, i: (rev(d, i), 2)),
            pl.BlockSpec((ROW_BLOCK, LANES), lambda d, i: (rev(d, i), small_col // LANES)),
            pl.BlockSpec((None, LANES, w), lambda d, i: (d, 0, 0)),
            pl.BlockSpec((None, LANES, w), lambda d, i: (d, 0, 0)),
            pl.BlockSpec((None, LANES, wc), lambda d, i: (d, 0, 0)),
            pl.BlockSpec((None, 1, LANES), lambda d, i: (l, 0, 0)),
            pl.BlockSpec((None, 1, LANES), lambda d, i: (l, 0, 0)),
            pl.BlockSpec((None, ROW_BLOCK, ROW_BLOCK), lambda d, i: (d, 0, 0)),
            pl.BlockSpec((ROW_BLOCK, ROW_BLOCK), lambda d, i: (0, 0)),
            pl.BlockSpec((None, ROW_BLOCK, wc), lambda d, i: (d, 0, 0)),
            pl.BlockSpec((ROW_BLOCK, wc), lambda d, i: (0, 0)),
            pl.BlockSpec((None, None, None, n_heads, LANES, LANES),
                         lambda d, i: (lay.smp_seq(rev(d, i)), l, d, 0, 0, 0)),
        ],
        out_specs=[
            pl.BlockSpec((None, ROW_BLOCK, w), lambda d, i: (d, rev(d, i), 0)),
            pl.BlockSpec((None, None, n_heads, LANES, LANES), lambda d, i: (lay.flags(rev(d, i))[3], d, 0, 0, 0)),
        ],
        out_shape=[
            jax.ShapeDtypeStruct((2, t, w), F32),
            jax.ShapeDtypeStruct((lay.nseq, 2, n_heads, LANES, LANES), F32),
        ],
        scratch_shapes=[
            pltpu.VMEM((n_heads // 2, 2 * LANES, 2 * LANES), F32),
            pltpu.VMEM((n_chunks, n_heads // 4, 4 * CHUNK, LANES), F32),
            pltpu.VMEM((n_chunks, n_heads // 4, 4 * CHUNK, LANES), F32),
            pltpu.VMEM((n_chunks, n_heads // 4, 4 * CHUNK, 4 * CHUNK), BF16),
            pltpu.VMEM((ROW_BLOCK, w), F32),
            pltpu.VMEM((ROW_BLOCK, w), F32),
        ],
        compiler_params=_params(("arbitrary", "arbitrary"), 40),
        name="deltanet_scan",
    )(conv, conv, conv, proj, e_beta, e_g, e_gc, bias, nega, tri, ones_bd, mask, eye, state)


def _gelu_tanh(x):
    return 0.5 * x * (1.0 + jnp.tanh(np.sqrt(2.0 / np.pi).astype(np.float32) * (x + 0.044715 * x * x * x)))


def _branch_body(ys_ref, xc_ref, z_ref, y5_ref, u_ref, od_ref, g_ref, dexp_ref, nssd_ref, s5d_ref, ndn_ref,
                 wglu_ref, o_ref):
    w = z_ref.shape[1]
    ya = (ys_ref[0] + ys_ref[1] + dexp_ref[...] * xc_ref[...]) * _silu(z_ref[...])
    ya = ya * lax.rsqrt(jnp.mean(ya * ya, axis=-1, keepdims=True) + NORM_EPS) * nssd_ref[...]
    o_ref[0] = ya.astype(BF16)

    yb = _gelu_tanh(y5_ref[0] + y5_ref[1] + s5d_ref[...] * u_ref[...])
    glu = _dot(yb.astype(BF16), wglu_ref[...])
    o_ref[1] = (glu[:, :w] * _sigmoid(glu[:, w:])).astype(BF16)

    gate = _silu(g_ref[...])
    for h in range(w // LANES):
        sl = slice(h * LANES, (h + 1) * LANES)
        o = od_ref[0, :, sl] + od_ref[1, :, sl]
        o = o * lax.rsqrt(jnp.mean(o * o, axis=-1, keepdims=True) + NORM_EPS) * ndn_ref[:, sl]
        o_ref[2, :, sl] = (o * gate[:, sl]).astype(BF16)


def _branch_call(lay, l, y_ssd, conv, x_col, proj, z_col, u_col, g_col, y_s5, o_dn, d_exp, n_ssd, s5_d, n_dn, w_glu):
    t = conv.shape[0]
    w = d_exp.shape[-1]
    vec = pl.BlockSpec((None, 1, w), lambda i: (l, 0, 0))
    pair = pl.BlockSpec((2, ROW_BLOCK, w), lambda i: (0, i, 0))
    return pl.pallas_call(
        _branch_body,
        grid=(lay.nblk,),
        in_specs=[
            pair,
            pl.BlockSpec((ROW_BLOCK, w), lambda i: (i, x_col // w)),
            pl.BlockSpec((ROW_BLOCK, w), lambda i: (i, z_col // w)),
            pair,
            pl.BlockSpec((ROW_BLOCK, w), lambda i: (i, u_col // w)),
            pair,
            pl.BlockSpec((ROW_BLOCK, w), lambda i: (i, g_col // w)),
            vec, vec, vec, vec,
            pl.BlockSpec((None, w, 2 * w), lambda i: (l, 0, 0)),
        ],
        out_specs=pl.BlockSpec((3, ROW_BLOCK, w), lambda i: (0, i, 0)),
        out_shape=jax.ShapeDtypeStruct((3, t, w), BF16),
        compiler_params=_params(("arbitrary",), 48),
        name="branch_epilogue",
    )(y_ssd, conv, proj, y_s5, proj, o_dn, proj, d_exp, n_ssd, s5_d, n_dn, w_glu)


def _merge_body(ys_ref, g0_ref, g1_ref, g2_ref, wb_ref, o_ref):
    acc = _sigmoid(g0_ref[...]) * _dot(ys_ref[0], wb_ref[0])
    acc = acc + _sigmoid(g1_ref[...]) * _dot(ys_ref[1], wb_ref[1])
    acc = acc + _sigmoid(g2_ref[...]) * _dot(ys_ref[2], wb_ref[2])
    o_ref[...] = acc.astype(BF16)


def _merge_call(lay, l, ys, proj, gate_col, w_branch):
    _, t, w = ys.shape
    d = w_branch.shape[-1]
    g0 = gate_col // d
    gates = [pl.BlockSpec((ROW_BLOCK, d), lambda i, k=k: (i, g0 + k)) for k in range(3)]
    return pl.pallas_call(
        _merge_body,
        grid=(lay.nblk,),
        in_specs=[pl.BlockSpec((3, ROW_BLOCK, w), lambda i: (0, i, 0))] + gates
        + [pl.BlockSpec((None, 3, w, d), lambda i: (l, 0, 0, 0))],
        out_specs=pl.BlockSpec((ROW_BLOCK, d), lambda i: (i, 0)),
        out_shape=jax.ShapeDtypeStruct((t, d), BF16),
        compiler_params=_params(("arbitrary",), 52),
        name="branch_merge",
    )(ys, proj, proj, proj, w_branch)


def _out_proj_body(m_ref, w_ref, x_ref, mod_ref, o_ref):
    o_ref[...] = x_ref[...] + mod_ref[2:3, :] * _dot(m_ref[...], w_ref[...])


def _out_proj_call(lay, l, merged, w_out, x, mod):
    t, d = x.shape
    tm = min(512, lay.l_smp)
    return pl.pallas_call(
        _out_proj_body,
        grid=(t // tm,),
        in_specs=[
            pl.BlockSpec((tm, d), lambda i: (i, 0)),
            pl.BlockSpec((None, d, d), lambda i: (l, 0, 0)),
            pl.BlockSpec((tm, d), lambda i: (i, 0)),
            pl.BlockSpec((None, None, 6, d), lambda i: (l, lay.cond_row(i, tm), 0, 0)),
        ],
        out_specs=pl.BlockSpec((tm, d), lambda i: (i, 0)),
        out_shape=jax.ShapeDtypeStruct((t, d), F32),
        compiler_params=_params(("arbitrary",), 48),
        name="out_proj",
    )(merged, w_out, x, mod)


def _router_body(x_ref, nw_ref, mod_ref, wr_ref, br_ref, lt_ref,
                 h_ref, e_ref, p_ref, r_ref, cnt_ref, run_ref, *, n_experts):
    i = pl.program_id(0)

    @pl.when(i == 0)
    def _():
        run_ref[...] = jnp.zeros_like(run_ref)

    h = _modulated_norm(x_ref[...], nw_ref[...], mod_ref[3:4, :], mod_ref[4:5, :])
    h_ref[...] = h
    logits = _dot(h, wr_ref[...], HIGHEST) + br_ref[...]
    rows = logits.shape[0]
    lane = lax.broadcasted_iota(jnp.int32, (rows, n_experts), 1).astype(F32)
    out_lane = lax.broadcasted_iota(jnp.int32, (rows, LANES), 1)
    cur = logits
    sels, tops, idxs = [], [], []
    for _ in range(TOP_K):
        m = jnp.max(cur, axis=-1, keepdims=True)
        idx = jnp.min(jnp.where(cur == m, lane, float(n_experts)), axis=-1, keepdims=True)
        sel = lane == idx
        sels.append(sel)
        tops.append(m)
        idxs.append(idx)
        cur = jnp.where(sel, -jnp.inf, cur)
    exps = [jnp.exp(m - tops[0]) for m in tops]
    denom = exps[0] + exps[1] + exps[2] + exps[3]
    onehot = jnp.zeros((rows, n_experts), F32)
    for sel in sels:
        onehot = onehot + sel.astype(F32)
    before = _dot(lt_ref[...], onehot.astype(BF16)) + run_ref[...]
    e_out = jnp.zeros((rows, LANES), F32)
    p_out = jnp.zeros((rows, LANES), F32)
    r_out = jnp.zeros((rows, LANES), F32)
    for k in range(TOP_K):
        rank = jnp.sum(jnp.where(sels[k], before, 0.0), axis=-1, keepdims=True)
        e_out = jnp.where(out_lane == k, idxs[k], e_out)
        p_out = jnp.where(out_lane == k, exps[k] / denom, p_out)
        r_out = jnp.where(out_lane == k, rank, r_out)
    e_ref[...] = e_out.astype(jnp.int32)
    p_ref[...] = p_out
    r_ref[...] = r_out.astype(jnp.int32)
    run_ref[...] = run_ref[...] + jnp.sum(onehot, axis=0, keepdims=True)
    cnt_ref[...] = jnp.broadcast_to(run_ref[...], cnt_ref.shape).astype(jnp.int32)


def _router_call(lay, l, x, norm_w, mod, w_router, b_router):
    t, d = x.shape
    n_experts = w_router.shape[-1]
    r = np.arange(ROW_BLOCK)
    lower = jnp.asarray((r[None, :] < r[:, None]).astype(np.float32), dtype=BF16)
    wide = pl.BlockSpec((ROW_BLOCK, LANES), lambda i: (i, 0))
    body = functools.partial(_router_body, n_experts=n_experts)
    return pl.pallas_call(
        body,
        grid=(lay.nblk,),
        in_specs=[
            pl.BlockSpec((ROW_BLOCK, d), lambda i: (i, 0)),
            pl.BlockSpec((None, 1, d), lambda i: (l, 0, 0)),
            pl.BlockSpec((None, None, 6, d), lambda i: (l, lay.cond_row(i, ROW_BLOCK), 0, 0)),
            pl.BlockSpec((None, d, n_experts), lambda i: (l, 0, 0)),
            pl.BlockSpec((None, 1, n_experts), lambda i: (l, 0, 0)),
            pl.BlockSpec((ROW_BLOCK, ROW_BLOCK), lambda i: (0, 0)),
        ],
        out_specs=[
            pl.BlockSpec((ROW_BLOCK, d), lambda i: (i, 0)),
            wide, wide, wide,
            pl.BlockSpec((SUBLANES, n_experts), lambda i: (0, 0)),
        ],
        out_shape=[
            jax.ShapeDtypeStruct((t, d), F32),
            jax.ShapeDtypeStruct((t, LANES), jnp.int32),
            jax.ShapeDtypeStruct((t, LANES), F32),
            jax.ShapeDtypeStruct((t, LANES), jnp.int32),
            jax.ShapeDtypeStruct((SUBLANES, n_experts), jnp.int32),
        ],
        scratch_shapes=[pltpu.VMEM((1, n_experts), F32)],
        compiler_params=_params(("arbitrary",), 32),
        name="moe_router",
    )(x, norm_w, mod, w_router, b_router, lower)


GATHER_UNROLL = 8


def _row_copy(src_hbm, src_row, dst, dst_row, sem):
    return pltpu.make_async_copy(src_hbm.at[pl.ds(src_row, 1)], dst.at[pl.ds(dst_row, 1)], sem)


def _start_row_gather(idx_ref, k, src_hbm, dst, sem, n_rows):
    def body(g, carry):
        for u in range(GATHER_UNROLL):
            r = g * GATHER_UNROLL + u
            _row_copy(src_hbm, idx_ref[0, k, r], dst, r, sem).start(priority=u % 2)
        return carry

    lax.fori_loop(0, n_rows // GATHER_UNROLL, body, 0)


def _wait_row_gather(src_hbm, dst, sem, n_rows):
    pltpu.make_async_copy(src_hbm.at[pl.ds(0, n_rows)], dst, sem).wait()


def _expert_gate_body(be_ref, nu_ref, fl_ref, tok_ref, nxt_ref, h_hbm, w_ref, b_ref, t_ref, xs_ref,
                      wb_ref, buf_ref, sem_ref, *, bm):
    i = pl.program_id(0)
    n_used = nu_ref[0]
    slot = lax.rem(i, 2)

    @pl.when(i == 0)
    def _():
        _start_row_gather(tok_ref, 0, h_hbm, buf_ref.at[0], sem_ref.at[0], bm)

    @pl.when(i + 1 < n_used)
    def _():
        _start_row_gather(nxt_ref, 0, h_hbm, buf_ref.at[1 - slot], sem_ref.at[1 - slot], bm)

    @pl.when(i < jnp.maximum(n_used, 1))
    def _():
        _wait_row_gather(h_hbm, buf_ref.at[slot], sem_ref.at[slot], bm)

    @pl.when((fl_ref[i] == 1) & (i < n_used))
    def _():
        wb_ref[...] = w_ref[...].astype(BF16)

    @pl.when(i < n_used)
    def _():
        x = buf_ref[slot].astype(BF16)
        xs_ref[...] = x
        glu = jnp.minimum(_dot(x, wb_ref[...]) + b_ref[...], SWIGLU_LIMIT)
        t_ref[...] = glu * _sigmoid(SWIGLU_ALPHA * glu)

    @pl.when(i >= n_used)
    def _():
        xs_ref[...] = jnp.zeros_like(xs_ref)
        t_ref[...] = jnp.zeros_like(t_ref)


def _expert_gate_call(l, block_e, n_used, first, row_tok, h2, w_up, b_up, bm):
    nb = block_e.shape[0]
    d = h2.shape[1]
    d_ff = w_up.shape[-1] // 2
    body = functools.partial(_expert_gate_body, bm=bm)
    grid_spec = pltpu.PrefetchScalarGridSpec(
        num_scalar_prefetch=3,
        grid=(nb,),
        in_specs=[
            pl.BlockSpec((1, 1, bm), lambda i, be, nu, fl: (i, 0, 0), memory_space=pltpu.SMEM),
            pl.BlockSpec((1, 1, bm), lambda i, be, nu, fl: (jnp.minimum(i + 1, nb - 1), 0, 0),
                         memory_space=pltpu.SMEM),
            pl.BlockSpec(memory_space=pl.ANY),
            pl.BlockSpec((None, None, d, d_ff), lambda i, be, nu, fl: (l, be[i], 0, 0)),
            pl.BlockSpec((None, None, 1, d_ff), lambda i, be, nu, fl: (l, be[i], 0, 0)),
        ],
        out_specs=[
            pl.BlockSpec((bm, d_ff), lambda i, be, nu, fl: (i, 0)),
            pl.BlockSpec((bm, d), lambda i, be, nu, fl: (i, 0)),
        ],
        scratch_shapes=[pltpu.VMEM((d, d_ff), BF16), pltpu.VMEM((2, bm, d), F32), pltpu.SemaphoreType.DMA((2,))],
    )
    return pl.pallas_call(
        body,
        grid_spec=grid_spec,
        out_shape=[jax.ShapeDtypeStruct((nb * bm, d_ff), F32), jax.ShapeDtypeStruct((nb * bm, d), BF16)],
        compiler_params=_params(("arbitrary",), 56),
        name="moe_expert_gate",
    )(block_e, n_used, first, row_tok, row_tok, h2, w_up, b_up)


def _expert_lin_body(be_ref, nu_ref, fl_ref, xs_ref, t_ref, w_ref, b_ref, o_ref, wb_ref):
    i = pl.program_id(0)
    n_used = nu_ref[0]

    @pl.when((fl_ref[i] == 1) & (i < n_used))
    def _():
        wb_ref[...] = w_ref[...].astype(BF16)

    @pl.when(i < n_used)
    def _():
        lin = jnp.clip(_dot(xs_ref[...], wb_ref[...]) + b_ref[...], -SWIGLU_LIMIT, SWIGLU_LIMIT)
        o_ref[...] = (t_ref[...] * (lin + 1.0)).astype(BF16)

    @pl.when(i >= n_used)
    def _():
        o_ref[...] = jnp.zeros_like(o_ref)


def _expert_lin_call(l, block_e, n_used, first, xs, t1, w_up, b_up, bm):
    nb = block_e.shape[0]
    d = xs.shape[1]
    d_ff = w_up.shape[-1] // 2
    grid_spec = pltpu.PrefetchScalarGridSpec(
        num_scalar_prefetch=3,
        grid=(nb,),
        in_specs=[
            pl.BlockSpec((bm, d), lambda i, be, nu, fl: (i, 0)),
            pl.BlockSpec((bm, d_ff), lambda i, be, nu, fl: (i, 0)),
            pl.BlockSpec((None, None, d, d_ff), lambda i, be, nu, fl: (l, be[i], 0, 1)),
            pl.BlockSpec((None, None, 1, d_ff), lambda i, be, nu, fl: (l, be[i], 0, 1)),
        ],
        out_specs=pl.BlockSpec((bm, d_ff), lambda i, be, nu, fl: (i, 0)),
        scratch_shapes=[pltpu.VMEM((d, d_ff), BF16)],
    )
    return pl.pallas_call(
        _expert_lin_body,
        grid_spec=grid_spec,
        out_shape=jax.ShapeDtypeStruct((nb * bm, d_ff), BF16),
        compiler_params=_params(("arbitrary",), 56),
        name="moe_expert_lin",
    )(block_e, n_used, first, xs, t1, w_up, b_up)


def _expert_down_body(be_ref, nu_ref, fl_ref, a_ref, w_ref, b_ref, o_ref, wb_ref):
    i = pl.program_id(0)
    n_used = nu_ref[0]

    @pl.when((fl_ref[i] == 1) & (i < n_used))
    def _():
        wb_ref[...] = w_ref[...].astype(BF16)

    @pl.when(i < n_used)
    def _():
        o_ref[...] = _dot(a_ref[...], wb_ref[...]) + b_ref[...]

    @pl.when(i >= n_used)
    def _():
        o_ref[...] = jnp.zeros_like(o_ref)


def _expert_down_call(l, block_e, n_used, first, act, w_down, b_down, bm):
    nb = block_e.shape[0]
    d_ff, d = w_down.shape[-2:]
    grid_spec = pltpu.PrefetchScalarGridSpec(
        num_scalar_prefetch=3,
        grid=(nb,),
        in_specs=[
            pl.BlockSpec((bm, d_ff), lambda i, be, nu, fl: (i, 0)),
            pl.BlockSpec((None, None, d_ff, d), lambda i, be, nu, fl: (l, be[i], 0, 0)),
            pl.BlockSpec((None, None, 1, d), lambda i, be, nu, fl: (l, be[i], 0, 0)),
        ],
        out_specs=pl.BlockSpec((bm, d), lambda i, be, nu, fl: (i, 0)),
        scratch_shapes=[pltpu.VMEM((d_ff, d), BF16)],
    )
    return pl.pallas_call(
        _expert_down_body,
        grid_spec=grid_spec,
        out_shape=jax.ShapeDtypeStruct((nb * bm, d), F32),
        compiler_params=_params(("arbitrary",), 56),
        name="moe_expert_down",
    )(block_e, n_used, first, act, w_down, b_down)


def _combine_body(pos_ref, nxt_ref, y_hbm, p_ref, x_ref, mod_ref, o_ref, buf_ref, sem_ref, *, nblk):
    i = pl.program_id(0)
    slot = lax.rem(i, 2)

    def start(idx_ref, s):
        for k in range(TOP_K):
            _start_row_gather(idx_ref, k, y_hbm, buf_ref.at[s, k], sem_ref.at[s], ROW_BLOCK)

    @pl.when(i == 0)
    def _():
        start(pos_ref, 0)

    @pl.when(i + 1 < nblk)
    def _():
        start(nxt_ref, 1 - slot)

    for k in range(TOP_K):
        _wait_row_gather(y_hbm, buf_ref.at[slot, k], sem_ref.at[slot], ROW_BLOCK)

    p = p_ref[...]
    acc = p[:, 0:1] * buf_ref[slot, 0]
    for k in range(1, TOP_K):
        acc = acc + p[:, k:k + 1] * buf_ref[slot, k]
    o_ref[...] = x_ref[...] + mod_ref[5:6, :] * acc


def _combine_call(lay, l, pos, y_sorted, top_p, x, mod):
    t, d = x.shape
    nblk = lay.nblk
    body = functools.partial(_combine_body, nblk=nblk)
    return pl.pallas_call(
        body,
        grid=(nblk,),
        in_specs=[
            pl.BlockSpec((1, TOP_K, ROW_BLOCK), lambda i: (i, 0, 0), memory_space=pltpu.SMEM),
            pl.BlockSpec((1, TOP_K, ROW_BLOCK), lambda i: (jnp.minimum(i + 1, nblk - 1), 0, 0),
                         memory_space=pltpu.SMEM),
            pl.BlockSpec(memory_space=pl.ANY),
            pl.BlockSpec((ROW_BLOCK, LANES), lambda i: (i, 0)),
            pl.BlockSpec((ROW_BLOCK, d), lambda i: (i, 0)),
            pl.BlockSpec((None, None, 6, d), lambda i: (l, lay.cond_row(i, ROW_BLOCK), 0, 0)),
        ],
        out_specs=pl.BlockSpec((ROW_BLOCK, d), lambda i: (i, 0)),
        out_shape=jax.ShapeDtypeStruct((t, d), F32),
        scratch_shapes=[pltpu.VMEM((2, TOP_K, ROW_BLOCK, d), F32), pltpu.SemaphoreType.DMA((2,))],
        compiler_params=_params(("arbitrary",), 40),
        name="moe_combine",
    )(pos, pos, y_sorted, top_p, x, mod)


def _moe(lay, l, x, norm_w, mod, w_router, b_router, w_up, b_up, w_down, b_down, bm):
    t = x.shape[0]
    n_experts = w_router.shape[-1]
    h2, top_e, top_p, rank, counts = _router_call(lay, l, x, norm_w, mod, w_router, b_router)
    counts = counts[0]
    padded = (counts + bm - 1) // bm * bm
    pad_end = jnp.cumsum(padded)
    pad_start = pad_end - padded
    e4 = top_e[:, :TOP_K]
    pos = pad_start[e4] + rank[:, :TOP_K]
    nb = t * TOP_K // bm + n_experts
    tok = jnp.broadcast_to(jnp.arange(t, dtype=jnp.int32)[:, None], (t, TOP_K))
    row_tok = jnp.zeros((nb * bm,), jnp.int32).at[pos.reshape(-1)].set(tok.reshape(-1))
    block_e = jnp.minimum(
        jnp.searchsorted(pad_end, jnp.arange(nb, dtype=jnp.int32) * bm, side="right"), n_experts - 1).astype(jnp.int32)
    n_used = (pad_end[-1] // bm).astype(jnp.int32).reshape(1)
    first = jnp.concatenate([jnp.ones((1,), jnp.int32), (block_e[1:] != block_e[:-1]).astype(jnp.int32)])
    t1, xs = _expert_gate_call(l, block_e, n_used, first, row_tok.reshape(nb, 1, bm), h2, w_up, b_up, bm)
    act = _expert_lin_call(l, block_e, n_used, first, xs, t1, w_up, b_up, bm)
    y_sorted = _expert_down_call(l, block_e, n_used, first, act, w_down, b_down, bm)
    pos3 = pos.reshape(lay.nblk, ROW_BLOCK, TOP_K).transpose(0, 2, 1)
    return _combine_call(lay, l, pos3, y_sorted, top_p, x, mod)


def _final_norm_body(x_ref, w_ref, o_ref):
    x = x_ref[...]
    o_ref[...] = x * lax.rsqrt(jnp.mean(x * x, axis=-1, keepdims=True) + NORM_EPS) * w_ref[...]


def _final_norm_call(x, w):
    t, d = x.shape
    tm = 512
    return pl.pallas_call(
        _final_norm_body,
        grid=(t // tm,),
        in_specs=[pl.BlockSpec((tm, d), lambda i: (i, 0)), pl.BlockSpec((1, d), lambda i: (0, 0))],
        out_specs=pl.BlockSpec((tm, d), lambda i: (i, 0)),
        out_shape=jax.ShapeDtypeStruct((t, d), F32),
        compiler_params=_params(("arbitrary",), 32),
        name="final_norm",
    )(x, w.reshape(1, d))


def _grid_pos(n_tok, d_model):
    rows = n_tok // GRID_W
    r = jnp.repeat(jnp.arange(rows, dtype=F32), GRID_W)
    col = jnp.tile(jnp.arange(GRID_W, dtype=F32), rows)
    quarter = d_model // 4
    omega = 1.0 / (10000.0 ** (jnp.arange(quarter, dtype=F32) / quarter))
    ang_r = r[:, None] * omega
    ang_c = col[:, None] * omega
    return jnp.concatenate([jnp.sin(ang_r), jnp.cos(ang_r), jnp.sin(ang_c), jnp.cos(ang_c)], axis=-1)


def _s5_discretise(lam_re, lam_im, log_step, b_re, b_im):
    step = jnp.exp(log_step)[..., None]
    ang = lam_im * step
    mag = jnp.exp(lam_re * step)
    abar_re = mag * jnp.cos(ang)
    abar_im = mag * jnp.sin(ang)
    num_re = abar_re - 1.0
    den = lam_re * lam_re + lam_im * lam_im
    coef_re = (num_re * lam_re + abar_im * lam_im) / den
    coef_im = (abar_im * lam_re - num_re * lam_im) / den
    bb_re = coef_re[..., None] * b_re[:, None] - coef_im[..., None] * b_im[:, None]
    bb_im = coef_re[..., None] * b_im[:, None] + coef_im[..., None] * b_re[:, None]
    return abar_re, abar_im, bb_re, bb_im


def _block_diag_chunks(m, groups_per_chunk):
    lead = m.shape[:-3]
    g, a, b = m.shape[-3:]
    nch = g // groups_per_chunk
    m = m.reshape(lead + (nch, groups_per_chunk, a, b))
    eye = jnp.eye(groups_per_chunk, dtype=m.dtype)
    out = jnp.einsum("...cgab,gh->...cgahb", m, eye)
    return out.reshape(lead + (nch, groups_per_chunk * a, groups_per_chunk * b))


def kernel(x_prompt, x_sample, state_ssd, state_s5, state_delta, c, c_ctx, norm1, norm2, w_ada, b_ada, w_in,
           ssd_conv_w, ssd_conv_b, ssd_a_log, ssd_dt_bias, ssd_d, ssd_norm, s5_lam_re, s5_lam_im, s5_log_step,
           s5_b_re, s5_b_im, s5_c_re, s5_c_im, s5_d, s5_w_glu, dn_conv_w, dn_a_log, dn_dt_bias, dn_norm,
           w_branch, w_out, w_router, b_router, w_up, b_up, w_down, b_down, final_norm):
    n_ctx, l_ctx, d = x_prompt.shape
    n_smp, l_smp, _ = x_sample.shape
    depth = norm1.shape[0]
    w = d // 2
    lay = _Layout(n_ctx, l_ctx, n_smp, l_smp)
    t = lay.t

    h_ssd = ssd_a_log.shape[-1]
    n_state = state_ssd.shape[-1]
    xbc_w = ssd_conv_w.shape[-1]
    bc_w = xbc_w - w
    gs, ps = s5_lam_re.shape[-2:]
    js = w // gs
    h_dn = dn_a_log.shape[-1]
    n_experts = w_router.shape[-1]
    assert w // h_ssd == CHUNK and n_state == LANES and w // h_dn == LANES and w % 512 == 0
    assert LANES % js == 0 and l_ctx % ROW_BLOCK == 0 and l_smp % ROW_BLOCK == 0
    assert n_ctx % SUBLANES == 0 and n_smp % SUBLANES == 0 and 2 * h_ssd + 4 * h_dn <= LANES
    assert n_smp + 1 <= 2 * SUBLANES

    o_z, o_xbc, o_dt = 0, w, w + xbc_w
    o_u = o_dt + 2 * h_ssd
    o_qkv = o_u + w
    o_g = o_qkv + 3 * w
    o_a = o_g + w
    o_b = o_a + 2 * h_dn
    o_gate = o_b + 2 * h_dn
    seg = lambda a, n: w_in[:, :, a:a + n]
    n_small = 2 * h_ssd + 4 * h_dn
    col_gate, col_z, col_u, col_g, col_qkv = 0, 3 * d, 3 * d + w, 3 * d + 2 * w, 3 * d + 3 * w
    col_small = col_qkv + 3 * w + xbc_w
    n_cols = col_small + LANES
    n_pad = -n_cols % 1024
    w_in_r = jnp.concatenate(
        [seg(o_gate, 3 * d), seg(o_z, w), seg(o_u, w), seg(o_g, w), seg(o_qkv, 3 * w), seg(o_xbc, xbc_w),
         seg(o_dt, 2 * h_ssd), seg(o_a, 2 * h_dn), seg(o_b, 2 * h_dn),
         jnp.zeros((depth, d, LANES - n_small + n_pad), F32)], axis=-1).astype(BF16)
    conv_w = jnp.concatenate([dn_conv_w, ssd_conv_w], axis=-1)
    conv_b = jnp.concatenate([jnp.zeros((depth, 3 * w), F32), ssd_conv_b], axis=-1)[:, None, :]
    conv_width = 3 * w + xbc_w
    small_bias = jnp.concatenate([ssd_dt_bias.reshape(depth, -1), dn_dt_bias.reshape(depth, -1),
                                  jnp.zeros((depth, LANES - 2 * h_ssd - 2 * h_dn), F32)], axis=-1)[:, None, :]

    ssd_aneg = jnp.repeat(-jnp.exp(ssd_a_log), CHUNK, axis=-1)[:, :, None, :]
    ssd_d_exp = jnp.repeat(ssd_d, CHUNK, axis=-1)[:, None, :]
    dn_nega = jnp.concatenate([jnp.zeros((depth, 2 * h_ssd), F32), -jnp.exp(dn_a_log).reshape(depth, -1),
                               jnp.zeros((depth, LANES - 2 * h_ssd - 2 * h_dn), F32)], axis=-1)[:, None, :]
    dn_norm_t = jnp.tile(dn_norm, (1, h_dn))[:, None, :]
    e_dt = _expand_matrix(lambda dd, h: dd * h_ssd + h, h_ssd, CHUNK)
    e_g = _expand_matrix(lambda dd, h: 2 * h_ssd + dd * h_dn + h, h_dn, LANES)
    e_gc = _expand_matrix(lambda dd, h: 2 * h_ssd + dd * h_dn + h, h_dn, CHUNK)
    e_beta = _expand_matrix(lambda dd, h: 2 * h_ssd + 2 * h_dn + dd * h_dn + h, h_dn, LANES)
    ssd_consts = _chunk_consts(w)
    dn_consts = _chunk_consts(h_dn * CHUNK)

    gpc = LANES // js
    abar_re, abar_im, bb_re, bb_im = _s5_discretise(s5_lam_re, s5_lam_im, s5_log_step, s5_b_re, s5_b_im)
    n_s5 = gs * ps
    abar_re = abar_re.reshape(depth, 2, 1, n_s5)
    abar_im = abar_im.reshape(depth, 2, 1, n_s5)
    bb_re = _block_diag_chunks(jnp.swapaxes(bb_re, -1, -2), gpc).astype(BF16)
    bb_im = _block_diag_chunks(jnp.swapaxes(bb_im, -1, -2), gpc).astype(BF16)
    cc_re = _block_diag_chunks(jnp.swapaxes(s5_c_re, -1, -2), gpc).astype(BF16)
    cc_im = _block_diag_chunks(jnp.swapaxes(s5_c_im, -1, -2), gpc).astype(BF16)

    ssd_state_t = jnp.swapaxes(state_ssd.reshape(n_smp, depth, 2, w, n_state), -1, -2)
    s5_state = state_s5.reshape(n_smp // SUBLANES, SUBLANES, depth, 2, 2, n_s5).transpose(0, 2, 3, 4, 1, 5)

    w_glu_b = s5_w_glu.astype(BF16)
    w_branch_b = w_branch.astype(BF16)
    w_out_b = w_out.astype(BF16)
    b_up4 = b_up[:, :, None, :]
    b_down4 = b_down[:, :, None, :]
    norm1_3 = norm1[:, None, :]
    norm2_3 = norm2[:, None, :]
    b_router3 = b_router[:, None, :]
    ssd_norm3 = ssd_norm[:, None, :]
    s5_d3 = s5_d[:, None, :]

    cond = jnp.zeros((2 * SUBLANES, d), F32).at[0].set(c_ctx).at[1:1 + n_smp].set(c)
    mod = _ada_call(cond, w_ada, b_ada).reshape(depth, 2 * SUBLANES, 6, d)

    xs = x_sample + _grid_pos(l_smp, d)[None]
    x = jnp.concatenate([x_prompt.reshape(lay.t_ctx, d), xs.reshape(n_smp * l_smp, d)], axis=0)

    def to_time_major(a):
        cw = a.shape[-1]
        ctx = a[:lay.t_ctx].reshape(n_ctx // SUBLANES, SUBLANES, l_ctx, cw).transpose(0, 2, 1, 3)
        smp = a[lay.t_ctx:].reshape(n_smp // SUBLANES, SUBLANES, l_smp, cw).transpose(0, 2, 1, 3)
        return jnp.concatenate([ctx.reshape(lay.t_ctx, cw), smp.reshape(t - lay.t_ctx, cw)], axis=0)

    def from_time_major(a):
        cw = a.shape[-1]
        lead = a.shape[:-2]
        ctx = a[..., :lay.t_ctx, :].reshape(lead + (n_ctx // SUBLANES, l_ctx, SUBLANES, cw))
        smp = a[..., lay.t_ctx:, :].reshape(lead + (n_smp // SUBLANES, l_smp, SUBLANES, cw))
        ctx = jnp.swapaxes(ctx, -2, -3).reshape(lead + (lay.t_ctx, cw))
        smp = jnp.swapaxes(smp, -2, -3).reshape(lead + (t - lay.t_ctx, cw))
        return jnp.concatenate([ctx, smp], axis=-2)

    bm = 256
    ssd_out, s5_out, dn_out = [], [], []
    for l in range(depth):
        proj = _in_proj_call(lay, l, x, norm1_3, mod, w_in_r)
        conv = _conv_call(lay, l, proj, conv_w, conv_b, col_qkv, conv_width, 2 * w)
        y_ssd, ssd_fin = _ssd_call(lay, l, conv, proj, col_small, 3 * w, 4 * w, ssd_consts, e_dt, small_bias,
                                   ssd_aneg, ssd_state_t)
        u_tm = to_time_major(proj[:, col_u:col_u + w])
        y_s5_tm, s5_fin = _s5_call(lay, l, u_tm, bb_re, bb_im, cc_re, cc_im, abar_re, abar_im, s5_state)
        y_s5 = from_time_major(y_s5_tm)
        o_dn, dn_fin = _dn_call(lay, l, conv, proj, col_small, dn_consts, e_beta, e_g, e_gc, small_bias,
                                dn_nega, state_delta)
        ys = _branch_call(lay, l, y_ssd, conv, 3 * w, proj, col_z, col_u, col_g, y_s5, o_dn,
                          ssd_d_exp, ssd_norm3, s5_d3, dn_norm_t, w_glu_b)
        merged = _merge_call(lay, l, ys, proj, col_gate, w_branch_b)
        x = _out_proj_call(lay, l, merged, w_out_b, x, mod)
        x = _moe(lay, l, x, norm2_3, mod, w_router, b_router3, w_up, b_up4, w_down, b_down4, bm)

        ssd_out.append(jnp.swapaxes(ssd_fin[:n_ctx], -1, -2).reshape(n_ctx, 2, h_ssd, CHUNK, n_state))
        ng = n_ctx // SUBLANES
        s5_out.append(s5_fin[:ng].transpose(0, 3, 1, 2, 4).reshape(n_ctx, 2, 2, gs, ps))
        dn_out.append(dn_fin[:n_ctx])

    y = _final_norm_call(x, final_norm)
    y_prompt = y[:lay.t_ctx].reshape(n_ctx, l_ctx, d)
    y_sample = y[lay.t_ctx:].reshape(n_smp, l_smp, d)
    return (y_prompt, y_sample, jnp.stack(ssd_out, axis=1), jnp.stack(s5_out, axis=1), jnp.stack(dn_out, axis=1))
```

```python
import functools

import numpy as np
import jax
import jax.numpy as jnp
from jax import lax
from jax.experimental import pallas as pl
from jax.experimental.pallas import tpu as pltpu

F32 = jnp.float32
BF16 = jnp.bfloat16
HIGHEST = lax.Precision.HIGHEST

ROW_BLOCK = 256
CHUNK = 64
LANES = 128
SUBLANES = 8
TOP_K = 4
NORM_EPS = 1e-6
L2_EPS = 1e-6
SWIGLU_ALPHA = 1.702
SWIGLU_LIMIT = 7.0
GRID_W = 64
NEG_BIG = -1e30
MIB = 1024 * 1024


def _params(sem, vmem_mib):
    return pltpu.CompilerParams(dimension_semantics=sem, vmem_limit_bytes=vmem_mib * MIB)


def _dot(a, b, precision=None):
    return jnp.dot(a, b, preferred_element_type=F32, precision=precision)


def _dot_nt(a, b):
    return lax.dot_general(a, b, (((1,), (1,)), ((), ())), preferred_element_type=F32)


def _dot_tn(a, b):
    return lax.dot_general(a, b, (((0,), (0,)), ((), ())), preferred_element_type=F32)


def _split_bf16(x, parts):
    out = []
    for n in range(parts):
        piece = x.astype(BF16)
        out.append(piece)
        if n + 1 < parts:
            x = x - piece.astype(F32)
    return out


def _dot_01_left(m01, x, parts=3):
    pieces = _split_bf16(x, parts)
    acc = _dot(m01, pieces[0])
    for piece in pieces[1:]:
        acc = acc + _dot(m01, piece)
    return acc


def _dot_01_right(x, m01, parts=3):
    pieces = _split_bf16(x, parts)
    acc = _dot(pieces[0], m01)
    for piece in pieces[1:]:
        acc = acc + _dot(piece, m01)
    return acc


def _sigmoid(x):
    return 1.0 / (1.0 + jnp.exp(-x))


def _silu(x):
    return x * _sigmoid(x)


def _softplus(x):
    return jnp.maximum(x, 0.0) + jnp.log1p(jnp.exp(-jnp.abs(x)))


class _Layout:
    def __init__(self, n_ctx, l_ctx, n_smp, l_smp):
        self.n_ctx, self.l_ctx, self.n_smp, self.l_smp = n_ctx, l_ctx, n_smp, l_smp
        self.t_ctx = n_ctx * l_ctx
        self.t = self.t_ctx + n_smp * l_smp
        self.nblk = self.t // ROW_BLOCK
        self.nblk_ctx = self.t_ctx // ROW_BLOCK
        self.bps_ctx = l_ctx // ROW_BLOCK
        self.bps_smp = l_smp // ROW_BLOCK
        self.nseq = n_ctx + n_smp

    def flags(self, ib):
        is_ctx = ib < self.nblk_ctx
        j = jnp.where(is_ctx, ib, ib - self.nblk_ctx)
        bps = jnp.where(is_ctx, self.bps_ctx, self.bps_smp)
        pos = lax.rem(j, bps)
        seq = jnp.where(is_ctx, lax.div(j, bps), self.n_ctx + lax.div(j, bps))
        return is_ctx, pos == 0, pos == bps - 1, seq

    def smp_seq(self, ib):
        j = jnp.maximum(ib - self.nblk_ctx, 0)
        return lax.div(j, self.bps_smp)

    def cond_row(self, i, rows):
        n_ctx_tiles = self.t_ctx // rows
        return jnp.where(i < n_ctx_tiles, 0, 1 + lax.div(jnp.maximum(i - n_ctx_tiles, 0), self.l_smp // rows))


def _ada_body(c_ref, w_ref, b_ref, o_ref):
    c = c_ref[...]
    o_ref[...] = _dot(_silu(c).astype(BF16), w_ref[...].astype(BF16)) + b_ref[...]


def _ada_call(cond, w_ada, b_ada):
    depth, d, n = w_ada.shape
    rows = cond.shape[0]
    tn = 1024
    return pl.pallas_call(
        _ada_body,
        grid=(depth, n // tn),
        in_specs=[
            pl.BlockSpec((rows, d), lambda l, j: (0, 0)),
            pl.BlockSpec((None, d, tn), lambda l, j: (l, 0, j)),
            pl.BlockSpec((None, 1, tn), lambda l, j: (l, 0, j)),
        ],
        out_specs=pl.BlockSpec((None, rows, tn), lambda l, j: (l, 0, j)),
        out_shape=jax.ShapeDtypeStruct((depth, rows, n), F32),
        compiler_params=_params(("arbitrary", "arbitrary"), 40),
        name="ada_mod",
    )(cond, w_ada, b_ada.reshape(depth, 1, n))


def _modulated_norm(x, nw, shift, scale):
    ms = jnp.mean(x * x, axis=-1, keepdims=True)
    y = x * lax.rsqrt(ms + NORM_EPS) * nw
    return y * (1.0 + scale) + shift


def _in_proj_body(x_ref, nw_ref, mod_ref, w_ref, o_ref, h_ref):
    @pl.when(pl.program_id(1) == 0)
    def _():
        h = _modulated_norm(x_ref[...], nw_ref[...], mod_ref[0:1, :], mod_ref[1:2, :])
        h_ref[...] = h.astype(BF16)

    o_ref[...] = _dot(h_ref[...], w_ref[...])


def _in_proj_call(lay, l, x, norm_w, mod, w_in):
    t, d = x.shape
    n = w_in.shape[-1]
    tm = min(1024, lay.l_smp)
    tn = 1024
    return pl.pallas_call(
        _in_proj_body,
        grid=(t // tm, n // tn),
        in_specs=[
            pl.BlockSpec((tm, d), lambda i, j: (i, 0)),
            pl.BlockSpec((None, 1, d), lambda i, j: (l, 0, 0)),
            pl.BlockSpec((None, None, 6, d), lambda i, j: (l, lay.cond_row(i, tm), 0, 0)),
            pl.BlockSpec((None, d, tn), lambda i, j: (l, 0, j)),
        ],
        out_specs=pl.BlockSpec((tm, tn), lambda i, j: (i, j)),
        out_shape=jax.ShapeDtypeStruct((t, n), F32),
        scratch_shapes=[pltpu.VMEM((tm, d), BF16)],
        compiler_params=_params(("arbitrary", "arbitrary"), 48),
        name="in_proj",
    )(x, norm_w, mod, w_in)


def _conv_body(x_ref, p_ref, n_ref, w_ref, b_ref, o_ref, ext_ref, *, lay, n_qk_blocks, conv_k):
    i = pl.program_id(0)
    c = pl.program_id(1)
    _, first, last, _ = lay.flags(i)
    halo = SUBLANES
    ext_ref[0:halo, :] = jnp.where(first, 0.0, p_ref[...])
    ext_ref[halo:halo + ROW_BLOCK, :] = x_ref[...]
    ext_ref[halo + ROW_BLOCK:2 * halo + ROW_BLOCK, :] = jnp.where(last, 0.0, n_ref[...])
    acc = b_ref[...] + jnp.zeros((ROW_BLOCK, x_ref.shape[1]), F32)
    for k in range(conv_k):
        off = halo - conv_k // 2 + k
        acc = acc + w_ref[k:k + 1, :] * ext_ref[off:off + ROW_BLOCK, :]
    y = _silu(acc)

    @pl.when(c < n_qk_blocks)
    def _():
        for hs in range(y.shape[1] // LANES):
            yh = y[:, hs * LANES:(hs + 1) * LANES]
            ss = jnp.sum(yh * yh, axis=-1, keepdims=True)
            o_ref[:, hs * LANES:(hs + 1) * LANES] = yh * lax.rsqrt(ss + L2_EPS)

    @pl.when(c >= n_qk_blocks)
    def _():
        o_ref[...] = y


def _conv_call(lay, l, proj, conv_w, conv_b, col0, width, qk_width):
    t = proj.shape[0]
    cb = 512
    conv_k = conv_w.shape[1]
    off = col0 // cb
    rb8 = ROW_BLOCK // SUBLANES
    body = functools.partial(_conv_body, lay=lay, n_qk_blocks=qk_width // cb, conv_k=conv_k)
    return pl.pallas_call(
        body,
        grid=(lay.nblk, width // cb),
        in_specs=[
            pl.BlockSpec((ROW_BLOCK, cb), lambda i, c: (i, off + c)),
            pl.BlockSpec((SUBLANES, cb), lambda i, c: (jnp.maximum(i * rb8 - 1, 0), off + c)),
            pl.BlockSpec((SUBLANES, cb), lambda i, c: (jnp.minimum((i + 1) * rb8, t // SUBLANES - 1), off + c)),
            pl.BlockSpec((None, conv_k, cb), lambda i, c: (l, 0, c)),
            pl.BlockSpec((None, 1, cb), lambda i, c: (l, 0, c)),
        ],
        out_specs=pl.BlockSpec((ROW_BLOCK, cb), lambda i, c: (i, c)),
        out_shape=jax.ShapeDtypeStruct((t, width), F32),
        scratch_shapes=[pltpu.VMEM((ROW_BLOCK + 2 * SUBLANES, cb), F32)],
        compiler_params=_params(("arbitrary", "arbitrary"), 24),
        name="dwconv_silu",
    )(proj, proj, proj, conv_w, conv_b)


def _chunk_consts(width):
    r = np.arange(ROW_BLOCK)
    same = (r[:, None] // CHUNK) == (r[None, :] // CHUNK)
    tri = np.stack([same & (r[None, :] <= r[:, None]), same & (r[None, :] >= r[:, None])]).astype(np.float32)
    ones_bd = same.astype(np.float32)
    s = np.arange(width) % CHUNK
    rr = r % CHUNK
    mask = np.stack([rr[:, None] >= s[None, :], rr[:, None] <= s[None, :]]).astype(np.float32)
    eye = (rr[:, None] == s[None, :]).astype(np.float32)
    return jnp.asarray(tri, dtype=BF16), jnp.asarray(ones_bd, dtype=BF16), jnp.asarray(mask), jnp.asarray(eye)


def _expand_matrix(col_of_head, n_heads, lanes_per_head):
    e = np.zeros((2, LANES, n_heads * lanes_per_head), np.float32)
    for d in range(2):
        for h in range(n_heads):
            e[d, col_of_head(d, h), h * lanes_per_head:(h + 1) * lanes_per_head] = 1.0
    return jnp.asarray(e, dtype=BF16)


def _half_masks():
    lane = lax.broadcasted_iota(jnp.int32, (1, LANES), 1)
    m0 = (lane < CHUNK).astype(F32)
    return m0, 1.0 - m0


def _ssd_body(*refs, **static):
    d = pl.program_id(0)

    @pl.when(d == 0)
    def _():
        _ssd_direction(*refs, rev=False, **static)

    @pl.when(d == 1)
    def _():
        _ssd_direction(*refs, rev=True, **static)


def _ssd_direction(x_ref, bc_ref, sm_ref, e_ref, bias_ref, aneg_ref, tri_ref, ones_ref, mask_ref, eye_ref, s0_ref,
                   y_ref, sout_ref, st_ref, cs_ref, ld_ref, xd_ref, *, rev, lay, n_groups, n_state,
                   heads_per_group):
    i = pl.program_id(1)
    ib = lay.nblk - 1 - i if rev else i
    is_ctx, first, last, _ = lay.flags(ib)
    start, end = (last, first) if rev else (first, last)

    @pl.when(start & is_ctx)
    def _():
        st_ref[...] = jnp.zeros_like(st_ref)

    @pl.when(start & jnp.logical_not(is_ctx))
    def _():
        st_ref[...] = s0_ref[...]

    dt = _dot_01_right(_softplus(sm_ref[...] + bias_ref[...]), e_ref[...])
    a = dt * aneg_ref[...]
    cs = _dot_01_left(tri_ref[...], a)
    cs_cols = _dot_01_left(ones_ref[...], eye_ref[...] * cs)
    cs_ref[...] = cs
    ld_ref[...] = jnp.exp(jnp.where(mask_ref[...] > 0.5, cs - cs_cols, NEG_BIG))
    xd_ref[...] = x_ref[...] * dt

    m0, m1 = _half_masks()
    n_chunks = ROW_BLOCK // CHUNK
    gw = heads_per_group * CHUNK
    for k in range(n_chunks):
        c = n_chunks - 1 - k if rev else k
        rows = slice(c * CHUNK, (c + 1) * CHUNK)
        cs_c = cs_ref[rows, :]
        ld_c = ld_ref[rows, :]
        xd_c = xd_ref[rows, :]
        bc_c = bc_ref[rows, :]
        cs_tot = cs_c[0:1, :] if rev else cs_c[CHUNK - 1:CHUNK, :]
        ecs = jnp.exp(cs_c)
        etot = jnp.exp(cs_tot)
        xdec = xd_c * jnp.exp(cs_tot - cs_c)
        for g in range(n_groups):
            bg = bc_c[:, g * n_state:(g + 1) * n_state].astype(BF16)
            cg = bc_c[:, (n_groups + g) * n_state:(n_groups + g + 1) * n_state].astype(BF16)
            cb2 = _dot_nt(cg, jnp.concatenate([bg, bg], axis=0))
            lo = g * gw
            st_g = st_ref[:, lo:lo + gw]
            y_off = _dot(cg, st_g.astype(BF16)) * ecs[:, lo:lo + gw]
            y_diag = []
            for jj in range(heads_per_group // 2):
                sl = slice(lo + jj * LANES, lo + (jj + 1) * LANES)
                m = (cb2 * ld_c[:, sl]).astype(BF16)
                xj = xd_c[:, sl]
                bd = jnp.concatenate([xj * m0, xj * m1], axis=0).astype(BF16)
                y_diag.append(_dot(m, bd))
            y_ref[rows, lo:lo + gw] = jnp.concatenate(y_diag, axis=1) + y_off
            st_ref[:, lo:lo + gw] = etot[:, lo:lo + gw] * st_g + _dot_tn(bg, xdec[:, lo:lo + gw].astype(BF16))

    @pl.when(end)
    def _():
        sout_ref[...] = st_ref[...]


def _ssd_call(lay, l, conv, proj, small_col, x_col, bc_col, consts, e_dt, bias, aneg, state_t):
    t = conv.shape[0]
    w = aneg.shape[-1]
    n_state = state_t.shape[-2]
    bcw = conv.shape[1] - x_col - w
    n_groups = bcw // (2 * n_state)
    heads = w // CHUNK
    tri, ones_bd, mask, eye = consts
    rev = lambda d, i: jnp.where(d == 0, i, lay.nblk - 1 - i)
    body = functools.partial(_ssd_body, lay=lay, n_groups=n_groups, n_state=n_state,
                             heads_per_group=heads // n_groups)
    return pl.pallas_call(
        body,
        grid=(2, lay.nblk),
        in_specs=[
            pl.BlockSpec((ROW_BLOCK, w), lambda d, i: (rev(d, i), x_col // w)),
            pl.BlockSpec((ROW_BLOCK, bcw), lambda d, i: (rev(d, i), bc_col // bcw)),
            pl.BlockSpec((ROW_BLOCK, LANES), lambda d, i: (rev(d, i), small_col // LANES)),
            pl.BlockSpec((None, LANES, w), lambda d, i: (d, 0, 0)),
            pl.BlockSpec((None, 1, LANES), lambda d, i: (l, 0, 0)),
            pl.BlockSpec((None, None, 1, w), lambda d, i: (l, d, 0, 0)),
            pl.BlockSpec((None, ROW_BLOCK, ROW_BLOCK), lambda d, i: (d, 0, 0)),
            pl.BlockSpec((ROW_BLOCK, ROW_BLOCK), lambda d, i: (0, 0)),
            pl.BlockSpec((None, ROW_BLOCK, w), lambda d, i: (d, 0, 0)),
            pl.BlockSpec((ROW_BLOCK, w), lambda d, i: (0, 0)),
            pl.BlockSpec((None, None, None, n_state, w), lambda d, i: (lay.smp_seq(rev(d, i)), l, d, 0, 0)),
        ],
        out_specs=[
            pl.BlockSpec((None, ROW_BLOCK, w), lambda d, i: (d, rev(d, i), 0)),
            pl.BlockSpec((None, None, n_state, w), lambda d, i: (lay.flags(rev(d, i))[3], d, 0, 0)),
        ],
        out_shape=[
            jax.ShapeDtypeStruct((2, t, w), F32),
            jax.ShapeDtypeStruct((lay.nseq, 2, n_state, w), F32),
        ],
        scratch_shapes=[pltpu.VMEM((n_state, w), F32)] + [pltpu.VMEM((ROW_BLOCK, w), F32)] * 3,
        compiler_params=_params(("arbitrary", "arbitrary"), 40),
        name="ssd_scan",
    )(conv, conv, proj, e_dt, bias, aneg, tri, ones_bd, mask, eye, state_t)


def _s5_body(u_ref, bbr_ref, bbi_ref, cr_ref, ci_ref, ar_ref, ai_ref, h0_ref,
             y_ref, hout_ref, bur_ref, bui_ref, hr_ref, hi_ref, *, nblk, nblk_ctx, bpg_ctx, bpg_smp):
    d = pl.program_id(0)
    i = pl.program_id(1)
    ib = jnp.where(d == 0, i, nblk - 1 - i)
    is_ctx = ib < nblk_ctx
    j = jnp.where(is_ctx, ib, ib - nblk_ctx)
    bpg = jnp.where(is_ctx, bpg_ctx, bpg_smp)
    pos = lax.rem(j, bpg)
    start = jnp.where(d == 0, pos == 0, pos == bpg - 1)
    end = jnp.where(d == 0, pos == bpg - 1, pos == 0)

    @pl.when(start & is_ctx)
    def _():
        hr_ref[...] = jnp.zeros_like(hr_ref)
        hi_ref[...] = jnp.zeros_like(hi_ref)

    @pl.when(start & jnp.logical_not(is_ctx))
    def _():
        hr_ref[...] = h0_ref[0]
        hi_ref[...] = h0_ref[1]

    n_lane_chunks = u_ref.shape[1] // LANES
    sw = bbr_ref.shape[-1]
    for cc in range(n_lane_chunks):
        uc = u_ref[:, cc * LANES:(cc + 1) * LANES].astype(BF16)
        bur_ref[:, cc * sw:(cc + 1) * sw] = _dot(uc, bbr_ref[cc])
        bui_ref[:, cc * sw:(cc + 1) * sw] = _dot(uc, bbi_ref[cc])

    n_steps = ROW_BLOCK // SUBLANES
    n_state = bur_ref.shape[1]
    seg = 512

    def step(s, carry):
        tstep = jnp.where(d == 0, s, n_steps - 1 - s)
        rows = pl.ds(pl.multiple_of(tstep * SUBLANES, SUBLANES), SUBLANES)
        for q in range(n_state // seg):
            sl = slice(q * seg, (q + 1) * seg)
            hr = hr_ref[:, sl]
            hi = hi_ref[:, sl]
            ar = ar_ref[:, sl]
            ai = ai_ref[:, sl]
            nr = ar * hr - ai * hi + bur_ref[rows, sl]
            ni = ar * hi + ai * hr + bui_ref[rows, sl]
            hr_ref[:, sl] = nr
            hi_ref[:, sl] = ni
            bur_ref[rows, sl] = nr
            bui_ref[rows, sl] = ni
        return carry

    lax.fori_loop(0, n_steps, step, 0)

    for cc in range(n_lane_chunks):
        hr = bur_ref[:, cc * sw:(cc + 1) * sw].astype(BF16)
        hi = bui_ref[:, cc * sw:(cc + 1) * sw].astype(BF16)
        y_ref[:, cc * LANES:(cc + 1) * LANES] = _dot(hr, cr_ref[cc]) - _dot(hi, ci_ref[cc])

    @pl.when(end)
    def _():
        hout_ref[0] = hr_ref[...]
        hout_ref[1] = hi_ref[...]


def _s5_call(lay, l, u_tm, bb_re, bb_im, c_re, c_im, abar_re, abar_im, h0):
    t, w = u_tm.shape
    n_state = abar_re.shape[-1]
    nch = w // LANES
    sw = n_state // nch
    ng_ctx = lay.n_ctx // SUBLANES
    ng_smp = lay.n_smp // SUBLANES
    bpg_ctx = lay.l_ctx * SUBLANES // ROW_BLOCK
    bpg_smp = lay.l_smp * SUBLANES // ROW_BLOCK
    rev = lambda d, i: jnp.where(d == 0, i, lay.nblk - 1 - i)

    def group(ib):
        is_ctx = ib < lay.nblk_ctx
        return jnp.where(is_ctx, lax.div(ib, bpg_ctx), ng_ctx + lax.div(jnp.maximum(ib - lay.nblk_ctx, 0), bpg_smp))

    def smp_group(ib):
        return lax.div(jnp.maximum(ib - lay.nblk_ctx, 0), bpg_smp)

    body = functools.partial(_s5_body, nblk=lay.nblk, nblk_ctx=lay.nblk_ctx, bpg_ctx=bpg_ctx, bpg_smp=bpg_smp)
    return pl.pallas_call(
        body,
        grid=(2, lay.nblk),
        in_specs=[
            pl.BlockSpec((ROW_BLOCK, w), lambda d, i: (rev(d, i), 0)),
            pl.BlockSpec((None, None, nch, LANES, sw), lambda d, i: (l, d, 0, 0, 0)),
            pl.BlockSpec((None, None, nch, LANES, sw), lambda d, i: (l, d, 0, 0, 0)),
            pl.BlockSpec((None, nch, sw, LANES), lambda d, i: (l, 0, 0, 0)),
            pl.BlockSpec((None, nch, sw, LANES), lambda d, i: (l, 0, 0, 0)),
            pl.BlockSpec((None, None, 1, n_state), lambda d, i: (l, d, 0, 0)),
            pl.BlockSpec((None, None, 1, n_state), lambda d, i: (l, d, 0, 0)),
            pl.BlockSpec((None, None, None, 2, SUBLANES, n_state),
                         lambda d, i: (smp_group(rev(d, i)), l, d, 0, 0, 0)),
        ],
        out_specs=[
            pl.BlockSpec((None, ROW_BLOCK, w), lambda d, i: (d, rev(d, i), 0)),
            pl.BlockSpec((None, None, 2, SUBLANES, n_state), lambda d, i: (group(rev(d, i)), d, 0, 0, 0)),
        ],
        out_shape=[
            jax.ShapeDtypeStruct((2, t, w), F32),
            jax.ShapeDtypeStruct((ng_ctx + ng_smp, 2, 2, SUBLANES, n_state), F32),
        ],
        scratch_shapes=[pltpu.VMEM((ROW_BLOCK, n_state), F32)] * 2 + [pltpu.VMEM((SUBLANES, n_state), F32)] * 2,
        compiler_params=_params(("arbitrary", "arbitrary"), 40),
        name="s5_scan",
    )(u_tm, bb_re, bb_im, c_re, c_im, abar_re, abar_im, h0)


def _dn_body(*refs, **static):
    d = pl.program_id(0)

    @pl.when(d == 0)
    def _():
        _dn_direction(*refs, rev=False, **static)

    @pl.when(d == 1)
    def _():
        _dn_direction(*refs, rev=True, **static)


def _dn_direction(q_ref, k_ref, v_ref, sm_ref, eb_ref, eg_ref, egc_ref, bias_ref, nega_ref,
                  tri_ref, ones_ref, mask_ref, eye_ref, s0_ref,
                  o_ref, sout_ref, s_ref, u_ref, w_ref, at_ref, eg_s, gc_s, *, rev, lay, n_heads):
    i = pl.program_id(1)
    ib = lay.nblk - 1 - i if rev else i
    is_ctx, first, last, _ = lay.flags(ib)
    start, end = (last, first) if rev else (first, last)
    n_pairs = n_heads // 2
    n_quads = n_heads // 4
    n_chunks = ROW_BLOCK // CHUNK
    dk = LANES
    pw = 2 * LANES
    qw = 4 * LANES
    qc = 4 * CHUNK

    @pl.when(start)
    def _():
        s_ref[...] = jnp.zeros_like(s_ref)

    @pl.when(start & jnp.logical_not(is_ctx))
    def _():
        for p in range(n_pairs):
            s_ref[p, 0:dk, 0:dk] = s0_ref[2 * p]
            s_ref[p, dk:pw, dk:pw] = s0_ref[2 * p + 1]

    scale = dk ** -0.5
    sm = sm_ref[...]
    g_small = _softplus(sm + bias_ref[...]) * nega_ref[...]
    gc_small = _dot_01_left(tri_ref[...], g_small)
    beta = _dot_01_right(_sigmoid(sm), eb_ref[...], parts=2)
    gc_wide = _dot_01_right(gc_small, eg_ref[...])
    gc_cols = _dot_01_right(gc_small, egc_ref[...])
    gc_t = _dot_01_left(ones_ref[...], eye_ref[...] * gc_cols)
    incl = mask_ref[...]
    dec = jnp.exp(jnp.where(incl > 0.5, gc_cols - gc_t, NEG_BIG))
    strict = incl - eye_ref[...]
    eg = jnp.exp(gc_wide)
    eg_s[...] = eg
    gc_s[...] = gc_wide

    lane_w = lax.broadcasted_iota(jnp.int32, (1, qw), 1)
    head_lanes = [(lax.shift_right_logical(lane_w, 7) == h).astype(F32) for h in range(4)]
    lane_c = lax.broadcasted_iota(jnp.int32, (1, qc), 1)
    head_cols = [(lax.shift_right_logical(lane_c, 6) == h).astype(F32) for h in range(4)]
    ri = lax.broadcasted_iota(jnp.int32, (qc, qc), 0)
    ci = lax.broadcasted_iota(jnp.int32, (qc, qc), 1)
    same = lambda s: lax.shift_right_logical(ri, s) == lax.shift_right_logical(ci, s)
    ident = (ri == ci).astype(F32)
    diag16 = same(4).astype(F32)
    off32 = (same(5) & jnp.logical_not(same(4))).astype(F32)
    off64 = (same(6) & jnp.logical_not(same(5))).astype(F32)
    pair_diag = same(7).astype(F32)

    def stack_heads(x, masks):
        return jnp.concatenate([x * m for m in masks], axis=0)

    scan_order = [n_chunks - 1 - k for k in range(n_chunks)] if rev else list(range(n_chunks))

    for k in scan_order:
        rs = slice(k * CHUNK, (k + 1) * CHUNK)
        for qd in range(n_quads):
            lw = slice(qd * qw, (qd + 1) * qw)
            lq = slice(qd * qc, (qd + 1) * qc)
            kk = k_ref[rs, lw]
            beta_q = beta[rs, lw]
            kb = kk * beta_q
            bdk = stack_heads(kk, head_lanes).astype(BF16)
            kbk = _dot_nt(kb.astype(BF16), bdk)
            dec_q = dec[rs, lq]
            a_bd = stack_heads(kbk * dec_q * strict[rs, lq], head_cols)
            p = -(a_bd * diag16)
            tinv = ident + p
            for _ in range(3):
                pb = p.astype(BF16)
                p = _dot(pb, pb)
                tinv = tinv + _dot(tinv.astype(BF16), p.astype(BF16))
            for off in (off32, off64):
                tb = tinv.astype(BF16)
                tinv = tinv - _dot(_dot(tb, (a_bd * off).astype(BF16)).astype(BF16), tb)
            vb = v_ref[rs, lw] * beta_q
            kbe = kb * eg[rs, lw]
            rhs = jnp.concatenate(
                [jnp.concatenate([vb[:, h * dk:(h + 1) * dk], kbe[:, h * dk:(h + 1) * dk]], axis=1) for h in range(4)],
                axis=0)
            sol = _dot(tinv.astype(BF16), rhs.astype(BF16))
            u_ref[k, qd] = sol[:, :dk]
            w_ref[k, qd] = sol[:, dk:]
            att = _dot_nt((q_ref[rs, lw] * scale).astype(BF16), bdk) * dec_q
            at_ref[k, qd] = stack_heads(att, head_cols).astype(BF16)

    for c in scan_order:
        rows = slice(c * CHUNK, (c + 1) * CHUNK)
        gc_c = gc_s[rows, :]
        eg_c = eg_s[rows, :]
        gc_tot = gc_c[0:1, :] if rev else gc_c[CHUNK - 1:CHUNK, :]
        g_tot = jnp.exp(gc_tot)
        kdec = jnp.exp(gc_tot - gc_c)
        for qd in range(n_quads):
            u_q = u_ref[c, qd]
            w_q = w_ref[c, qd]
            vns, qss, vn_rows = [], [], []
            for pp in range(2):
                p = 2 * qd + pp
                lp = slice(p * pw, (p + 1) * pw)
                ra = slice(pp * LANES, pp * LANES + CHUNK)
                rb = slice(pp * LANES + CHUNK, (pp + 1) * LANES)
                wab = jnp.concatenate([w_q[ra], w_q[rb]], axis=1)
                qdec = q_ref[rows, lp] * scale * eg_c[:, lp]
                res = _dot(jnp.concatenate([wab, qdec], axis=0).astype(BF16), s_ref[p].astype(BF16))
                vn = jnp.concatenate([u_q[ra], u_q[rb]], axis=1) - res[:CHUNK]
                vns.append(vn)
                qss.append(res[CHUNK:])
                vn_rows += [vn[:, :dk], vn[:, dk:]]
            o_q = _dot(at_ref[c, qd], jnp.concatenate(vn_rows, axis=0).astype(BF16))
            for pp in range(2):
                p = 2 * qd + pp
                lp = slice(p * pw, (p + 1) * pw)
                ra = slice(pp * LANES, pp * LANES + CHUNK)
                rb = slice(pp * LANES + CHUNK, (pp + 1) * LANES)
                o_ref[rows, lp] = qss[pp] + jnp.concatenate([o_q[ra], o_q[rb]], axis=1)
                kd = (k_ref[rows, lp] * kdec[:, lp]).astype(BF16)
                s_ref[p] = s_ref[p] * g_tot[:, lp] + _dot_tn(kd, vns[pp].astype(BF16)) * pair_diag

    @pl.when(end)
    def _():
        for p in range(n_pairs):
            sout_ref[2 * p] = s_ref[p, 0:dk, 0:dk]
            sout_ref[2 * p + 1] = s_ref[p, dk:pw, dk:pw]


def _dn_call(lay, l, conv, proj, small_col, consts, e_beta, e_g, e_gc, bias, nega, state):
    t = conv.shape[0]
    w = e_beta.shape[-1]
    n_heads = w // LANES
    assert n_heads % 4 == 0
    wc = n_heads * CHUNK
    tri, ones_bd, mask, eye = consts
    n_chunks = ROW_BLOCK // CHUNK
    rev = lambda d, i: jnp.where(d == 0, i, lay.nblk - 1 - i)
    body = functools.partial(_dn_body, lay=lay, n_heads=n_heads)
    return pl.pallas_call(
        body,
        grid=(2, lay.nblk),
        in_specs=[
            pl.BlockSpec((ROW_BLOCK, w), lambda d, i: (rev(d, i), 0)),
            pl.BlockSpec((ROW_BLOCK, w), lambda d, i: (rev(d, i), 1)),
            pl.BlockSpec((ROW_BLOCK, w), lambda d, i: (rev(d, i), 2)),
            pl.BlockSpec((ROW_BLOCK, LANES), lambda d, i: (rev(d, i), small_col // LANES)),
            pl.BlockSpec((None, LANES, w), lambda d, i: (d, 0, 0)),
            pl.BlockSpec((None, LANES, w), lambda d, i: (d, 0, 0)),
            pl.BlockSpec((None, LANES, wc), lambda d, i: (d, 0, 0)),
            pl.BlockSpec((None, 1, LANES), lambda d, i: (l, 0, 0)),
            pl.BlockSpec((None, 1, LANES), lambda d, i: (l, 0, 0)),
            pl.BlockSpec((None, ROW_BLOCK, ROW_BLOCK), lambda d, i: (d, 0, 0)),
            pl.BlockSpec((ROW_BLOCK, ROW_BLOCK), lambda d, i: (0, 0)),
            pl.BlockSpec((None, ROW_BLOCK, wc), lambda d, i: (d, 0, 0)),
            pl.BlockSpec((ROW_BLOCK, wc), lambda d, i: (0, 0)),
            pl.BlockSpec((None, None, None, n_heads, LANES, LANES),
                         lambda d, i: (lay.smp_seq(rev(d, i)), l, d, 0, 0, 0)),
        ],
        out_specs=[
            pl.BlockSpec((None, ROW_BLOCK, w), lambda d, i: (d, rev(d, i), 0)),
            pl.BlockSpec((None, None, n_heads, LANES, LANES), lambda d, i: (lay.flags(rev(d, i))[3], d, 0, 0, 0)),
        ],
        out_shape=[
            jax.ShapeDtypeStruct((2, t, w), F32),
            jax.ShapeDtypeStruct((lay.nseq, 2, n_heads, LANES, LANES), F32),
        ],
        scratch_shapes=[
            pltpu.VMEM((n_heads // 2, 2 * LANES, 2 * LANES), F32),
            pltpu.VMEM((n_chunks, n_heads // 4, 4 * CHUNK, LANES), F32),
            pltpu.VMEM((n_chunks, n_heads // 4, 4 * CHUNK, LANES), F32),
            pltpu.VMEM((n_chunks, n_heads // 4, 4 * CHUNK, 4 * CHUNK), BF16),
            pltpu.VMEM((ROW_BLOCK, w), F32),
            pltpu.VMEM((ROW_BLOCK, w), F32),
        ],
        compiler_params=_params(("arbitrary", "arbitrary"), 40),
        name="deltanet_scan",
    )(conv, conv, conv, proj, e_beta, e_g, e_gc, bias, nega, tri, ones_bd, mask, eye, state)


def _gelu_tanh(x):
    return 0.5 * x * (1.0 + jnp.tanh(np.sqrt(2.0 / np.pi).astype(np.float32) * (x + 0.044715 * x * x * x)))


def _branch_body(ys_ref, xc_ref, z_ref, y5_ref, u_ref, od_ref, g_ref, dexp_ref, nssd_ref, s5d_ref, ndn_ref,
                 wglu_ref, o_ref):
    w = z_ref.shape[1]
    ya = (ys_ref[0] + ys_ref[1] + dexp_ref[...] * xc_ref[...]) * _silu(z_ref[...])
    ya = ya * lax.rsqrt(jnp.mean(ya * ya, axis=-1, keepdims=True) + NORM_EPS) * nssd_ref[...]
    o_ref[0] = ya.astype(BF16)

    yb = _gelu_tanh(y5_ref[0] + y5_ref[1] + s5d_ref[...] * u_ref[...])
    glu = _dot(yb.astype(BF16), wglu_ref[...])
    o_ref[1] = (glu[:, :w] * _sigmoid(glu[:, w:])).astype(BF16)

    gate = _silu(g_ref[...])
    for h in range(w // LANES):
        sl = slice(h * LANES, (h + 1) * LANES)
        o = od_ref[0, :, sl] + od_ref[1, :, sl]
        o = o * lax.rsqrt(jnp.mean(o * o, axis=-1, keepdims=True) + NORM_EPS) * ndn_ref[:, sl]
        o_ref[2, :, sl] = (o * gate[:, sl]).astype(BF16)


def _branch_call(lay, l, y_ssd, conv, x_col, proj, z_col, u_col, g_col, y_s5, o_dn, d_exp, n_ssd, s5_d, n_dn, w_glu):
    t = conv.shape[0]
    w = d_exp.shape[-1]
    vec = pl.BlockSpec((None, 1, w), lambda i: (l, 0, 0))
    pair = pl.BlockSpec((2, ROW_BLOCK, w), lambda i: (0, i, 0))
    return pl.pallas_call(
        _branch_body,
        grid=(lay.nblk,),
        in_specs=[
            pair,
            pl.BlockSpec((ROW_BLOCK, w), lambda i: (i, x_col // w)),
            pl.BlockSpec((ROW_BLOCK, w), lambda i: (i, z_col // w)),
            pair,
            pl.BlockSpec((ROW_BLOCK, w), lambda i: (i, u_col // w)),
            pair,
            pl.BlockSpec((ROW_BLOCK, w), lambda i: (i, g_col // w)),
            vec, vec, vec, vec,
            pl.BlockSpec((None, w, 2 * w), lambda i: (l, 0, 0)),
        ],
        out_specs=pl.BlockSpec((3, ROW_BLOCK, w), lambda i: (0, i, 0)),
        out_shape=jax.ShapeDtypeStruct((3, t, w), BF16),
        compiler_params=_params(("arbitrary",), 48),
        name="branch_epilogue",
    )(y_ssd, conv, proj, y_s5, proj, o_dn, proj, d_exp, n_ssd, s5_d, n_dn, w_glu)


def _merge_body(ys_ref, g0_ref, g1_ref, g2_ref, wb_ref, o_ref):
    acc = _sigmoid(g0_ref[...]) * _dot(ys_ref[0], wb_ref[0])
    acc = acc + _sigmoid(g1_ref[...]) * _dot(ys_ref[1], wb_ref[1])
    acc = acc + _sigmoid(g2_ref[...]) * _dot(ys_ref[2], wb_ref[2])
    o_ref[...] = acc.astype(BF16)


def _merge_call(lay, l, ys, proj, gate_col, w_branch):
    _, t, w = ys.shape
    d = w_branch.shape[-1]
    g0 = gate_col // d
    gates = [pl.BlockSpec((ROW_BLOCK, d), lambda i, k=k: (i, g0 + k)) for k in range(3)]
    return pl.pallas_call(
        _merge_body,
        grid=(lay.nblk,),
        in_specs=[pl.BlockSpec((3, ROW_BLOCK, w), lambda i: (0, i, 0))] + gates
        + [pl.BlockSpec((None, 3, w, d), lambda i: (l, 0, 0, 0))],
        out_specs=pl.BlockSpec((ROW_BLOCK, d), lambda i: (i, 0)),
        out_shape=jax.ShapeDtypeStruct((t, d), BF16),
        compiler_params=_params(("arbitrary",), 52),
        name="branch_merge",
    )(ys, proj, proj, proj, w_branch)


def _out_proj_body(m_ref, w_ref, x_ref, mod_ref, o_ref):
    o_ref[...] = x_ref[...] + mod_ref[2:3, :] * _dot(m_ref[...], w_ref[...])


def _out_proj_call(lay, l, merged, w_out, x, mod):
    t, d = x.shape
    tm = min(512, lay.l_smp)
    return pl.pallas_call(
        _out_proj_body,
        grid=(t // tm,),
        in_specs=[
            pl.BlockSpec((tm, d), lambda i: (i, 0)),
            pl.BlockSpec((None, d, d), lambda i: (l, 0, 0)),
            pl.BlockSpec((tm, d), lambda i: (i, 0)),
            pl.BlockSpec((None, None, 6, d), lambda i: (l, lay.cond_row(i, tm), 0, 0)),
        ],
        out_specs=pl.BlockSpec((tm, d), lambda i: (i, 0)),
        out_shape=jax.ShapeDtypeStruct((t, d), F32),
        compiler_params=_params(("arbitrary",), 48),
        name="out_proj",
    )(merged, w_out, x, mod)


def _router_body(x_ref, nw_ref, mod_ref, wr_ref, br_ref, lt_ref,
                 h_ref, e_ref, p_ref, r_ref, cnt_ref, run_ref, *, n_experts):
    i = pl.program_id(0)

    @pl.when(i == 0)
    def _():
        run_ref[...] = jnp.zeros_like(run_ref)

    h = _modulated_norm(x_ref[...], nw_ref[...], mod_ref[3:4, :], mod_ref[4:5, :])
    h_ref[...] = h
    logits = _dot(h, wr_ref[...], HIGHEST) + br_ref[...]
    rows = logits.shape[0]
    lane = lax.broadcasted_iota(jnp.int32, (rows, n_experts), 1).astype(F32)
    out_lane = lax.broadcasted_iota(jnp.int32, (rows, LANES), 1)
    cur = logits
    sels, tops, idxs = [], [], []
    for _ in range(TOP_K):
        m = jnp.max(cur, axis=-1, keepdims=True)
        idx = jnp.min(jnp.where(cur == m, lane, float(n_experts)), axis=-1, keepdims=True)
        sel = lane == idx
        sels.append(sel)
        tops.append(m)
        idxs.append(idx)
        cur = jnp.where(sel, -jnp.inf, cur)
    exps = [jnp.exp(m - tops[0]) for m in tops]
    denom = exps[0] + exps[1] + exps[2] + exps[3]
    onehot = jnp.zeros((rows, n_experts), F32)
    for sel in sels:
        onehot = onehot + sel.astype(F32)
    before = _dot(lt_ref[...], onehot.astype(BF16)) + run_ref[...]
    e_out = jnp.zeros((rows, LANES), F32)
    p_out = jnp.zeros((rows, LANES), F32)
    r_out = jnp.zeros((rows, LANES), F32)
    for k in range(TOP_K):
        rank = jnp.sum(jnp.where(sels[k], before, 0.0), axis=-1, keepdims=True)
        e_out = jnp.where(out_lane == k, idxs[k], e_out)
        p_out = jnp.where(out_lane == k, exps[k] / denom, p_out)
        r_out = jnp.where(out_lane == k, rank, r_out)
    e_ref[...] = e_out.astype(jnp.int32)
    p_ref[...] = p_out
    r_ref[...] = r_out.astype(jnp.int32)
    run_ref[...] = run_ref[...] + jnp.sum(onehot, axis=0, keepdims=True)
    cnt_ref[...] = jnp.broadcast_to(run_ref[...], cnt_ref.shape).astype(jnp.int32)


def _router_call(lay, l, x, norm_w, mod, w_router, b_router):
    t, d = x.shape
    n_experts = w_router.shape[-1]
    r = np.arange(ROW_BLOCK)
    lower = jnp.asarray((r[None, :] < r[:, None]).astype(np.float32), dtype=BF16)
    wide = pl.BlockSpec((ROW_BLOCK, LANES), lambda i: (i, 0))
    body = functools.partial(_router_body, n_experts=n_experts)
    return pl.pallas_call(
        body,
        grid=(lay.nblk,),
        in_specs=[
            pl.BlockSpec((ROW_BLOCK, d), lambda i: (i, 0)),
            pl.BlockSpec((None, 1, d), lambda i: (l, 0, 0)),
            pl.BlockSpec((None, None, 6, d), lambda i: (l, lay.cond_row(i, ROW_BLOCK), 0, 0)),
            pl.BlockSpec((None, d, n_experts), lambda i: (l, 0, 0)),
            pl.BlockSpec((None, 1, n_experts), lambda i: (l, 0, 0)),
            pl.BlockSpec((ROW_BLOCK, ROW_BLOCK), lambda i: (0, 0)),
        ],
        out_specs=[
            pl.BlockSpec((ROW_BLOCK, d), lambda i: (i, 0)),
            wide, wide, wide,
            pl.BlockSpec((SUBLANES, n_experts), lambda i: (0, 0)),
        ],
        out_shape=[
            jax.ShapeDtypeStruct((t, d), F32),
            jax.ShapeDtypeStruct((t, LANES), jnp.int32),
            jax.ShapeDtypeStruct((t, LANES), F32),
            jax.ShapeDtypeStruct((t, LANES), jnp.int32),
            jax.ShapeDtypeStruct((SUBLANES, n_experts), jnp.int32),
        ],
        scratch_shapes=[pltpu.VMEM((1, n_experts), F32)],
        compiler_params=_params(("arbitrary",), 32),
        name="moe_router",
    )(x, norm_w, mod, w_router, b_router, lower)


GATHER_UNROLL = 8


def _row_copy(src_hbm, src_row, dst, dst_row, sem):
    return pltpu.make_async_copy(src_hbm.at[pl.ds(src_row, 1)], dst.at[pl.ds(dst_row, 1)], sem)


def _start_row_gather(idx_ref, k, src_hbm, dst, sem, n_rows):
    def body(g, carry):
        for u in range(GATHER_UNROLL):
            r = g * GATHER_UNROLL + u
            _row_copy(src_hbm, idx_ref[0, k, r], dst, r, sem).start(priority=u % 2)
        return carry

    lax.fori_loop(0, n_rows // GATHER_UNROLL, body, 0)


def _wait_row_gather(src_hbm, dst, sem, n_rows):
    pltpu.make_async_copy(src_hbm.at[pl.ds(0, n_rows)], dst, sem).wait()


def _cast_weights_on_expert_change(fl_ref, n_used, pairs):
    i = pl.program_id(0)

    @pl.when((fl_ref[i] == 1) & (i < n_used))
    def _():
        for w_ref, wb_ref in pairs:
            wb_ref[...] = w_ref[...].astype(BF16)


def _swiglu_half(x, wg_ref, wl_ref, bg_ref, bl_ref):
    glu = jnp.minimum(_dot(x, wg_ref[...]) + bg_ref[...], SWIGLU_LIMIT)
    lin = jnp.clip(_dot(x, wl_ref[...]) + bl_ref[...], -SWIGLU_LIMIT, SWIGLU_LIMIT)
    return (glu * _sigmoid(SWIGLU_ALPHA * glu) * (lin + 1.0)).astype(BF16)


def _expert_up_gather_body(be_ref, nu_ref, fl_ref, tok_ref, nxt_ref, h_hbm, wg_ref, wl_ref, bg_ref, bl_ref,
                           act_ref, xs_ref, wgb_ref, wlb_ref, buf_ref, sem_ref, *, bm):
    i = pl.program_id(0)
    n_used = nu_ref[0]
    slot = lax.rem(i, 2)

    @pl.when(i == 0)
    def _():
        _start_row_gather(tok_ref, 0, h_hbm, buf_ref.at[0], sem_ref.at[0], bm)

    @pl.when(i + 1 < n_used)
    def _():
        _start_row_gather(nxt_ref, 0, h_hbm, buf_ref.at[1 - slot], sem_ref.at[1 - slot], bm)

    @pl.when(i < jnp.maximum(n_used, 1))
    def _():
        _wait_row_gather(h_hbm, buf_ref.at[slot], sem_ref.at[slot], bm)

    _cast_weights_on_expert_change(fl_ref, n_used, ((wg_ref, wgb_ref), (wl_ref, wlb_ref)))

    @pl.when(i < n_used)
    def _():
        x = buf_ref[slot].astype(BF16)
        xs_ref[...] = x
        act_ref[...] = _swiglu_half(x, wgb_ref, wlb_ref, bg_ref, bl_ref)

    @pl.when(i >= n_used)
    def _():
        xs_ref[...] = jnp.zeros_like(xs_ref)
        act_ref[...] = jnp.zeros_like(act_ref)


def _expert_up_body(be_ref, nu_ref, fl_ref, xs_ref, wg_ref, wl_ref, bg_ref, bl_ref, act_ref, wgb_ref, wlb_ref):
    i = pl.program_id(0)
    n_used = nu_ref[0]
    _cast_weights_on_expert_change(fl_ref, n_used, ((wg_ref, wgb_ref), (wl_ref, wlb_ref)))

    @pl.when(i < n_used)
    def _():
        act_ref[...] = _swiglu_half(xs_ref[...], wgb_ref, wlb_ref, bg_ref, bl_ref)

    @pl.when(i >= n_used)
    def _():
        act_ref[...] = jnp.zeros_like(act_ref)


def _expert_up_call(l, block_e, n_used, first, half, rows, w_up, b_up, bm):
    nb = block_e.shape[0]
    d = w_up.shape[-2]
    d_ff = w_up.shape[-1] // 2
    n_half = 2
    fh = d_ff // n_half
    w_specs = [
        pl.BlockSpec((None, None, d, fh), lambda i, be, nu, fl: (l, be[i], 0, half)),
        pl.BlockSpec((None, None, d, fh), lambda i, be, nu, fl: (l, be[i], 0, n_half + half)),
        pl.BlockSpec((None, None, 1, fh), lambda i, be, nu, fl: (l, be[i], 0, half)),
        pl.BlockSpec((None, None, 1, fh), lambda i, be, nu, fl: (l, be[i], 0, n_half + half)),
    ]
    act_spec = pl.BlockSpec((bm, fh), lambda i, be, nu, fl: (i, 0))
    xs_spec = pl.BlockSpec((bm, d), lambda i, be, nu, fl: (i, 0))
    act_shape = jax.ShapeDtypeStruct((nb * bm, fh), BF16)
    w_scratch = [pltpu.VMEM((d, fh), BF16), pltpu.VMEM((d, fh), BF16)]
    if half == 0:
        row_tok, h2 = rows
        grid_spec = pltpu.PrefetchScalarGridSpec(
            num_scalar_prefetch=3,
            grid=(nb,),
            in_specs=[
                pl.BlockSpec((1, 1, bm), lambda i, be, nu, fl: (i, 0, 0), memory_space=pltpu.SMEM),
                pl.BlockSpec((1, 1, bm), lambda i, be, nu, fl: (jnp.minimum(i + 1, nb - 1), 0, 0),
                             memory_space=pltpu.SMEM),
                pl.BlockSpec(memory_space=pl.ANY),
            ] + w_specs,
            out_specs=[act_spec, xs_spec],
            scratch_shapes=w_scratch + [pltpu.VMEM((2, bm, d), F32), pltpu.SemaphoreType.DMA((2,))],
        )
        return pl.pallas_call(
            functools.partial(_expert_up_gather_body, bm=bm),
            grid_spec=grid_spec,
            out_shape=[act_shape, jax.ShapeDtypeStruct((nb * bm, d), BF16)],
            compiler_params=_params(("arbitrary",), 56),
            name="moe_expert_up_gather",
        )(block_e, n_used, first, row_tok, row_tok, h2, w_up, w_up, b_up, b_up)
    grid_spec = pltpu.PrefetchScalarGridSpec(
        num_scalar_prefetch=3,
        grid=(nb,),
        in_specs=[xs_spec] + w_specs,
        out_specs=act_spec,
        scratch_shapes=w_scratch,
    )
    return pl.pallas_call(
        _expert_up_body,
        grid_spec=grid_spec,
        out_shape=act_shape,
        compiler_params=_params(("arbitrary",), 56),
        name="moe_expert_up",
    )(block_e, n_used, first, rows, w_up, w_up, b_up, b_up)


def _expert_down_body(be_ref, nu_ref, fl_ref, a0_ref, a1_ref, w0_ref, w1_ref, b_ref, o_ref, wb0_ref, wb1_ref):
    i = pl.program_id(0)
    n_used = nu_ref[0]
    _cast_weights_on_expert_change(fl_ref, n_used, ((w0_ref, wb0_ref), (w1_ref, wb1_ref)))

    @pl.when(i < n_used)
    def _():
        o_ref[...] = _dot(a0_ref[...], wb0_ref[...]) + _dot(a1_ref[...], wb1_ref[...]) + b_ref[...]

    @pl.when(i >= n_used)
    def _():
        o_ref[...] = jnp.zeros_like(o_ref)


def _expert_down_call(l, block_e, n_used, first, act0, act1, w_down, b_down, bm):
    nb = block_e.shape[0]
    d_ff, d = w_down.shape[-2:]
    fh = act0.shape[1]
    act_spec = pl.BlockSpec((bm, fh), lambda i, be, nu, fl: (i, 0))
    grid_spec = pltpu.PrefetchScalarGridSpec(
        num_scalar_prefetch=3,
        grid=(nb,),
        in_specs=[
            act_spec, act_spec,
            pl.BlockSpec((None, None, fh, d), lambda i, be, nu, fl: (l, be[i], 0, 0)),
            pl.BlockSpec((None, None, fh, d), lambda i, be, nu, fl: (l, be[i], 1, 0)),
            pl.BlockSpec((None, None, 1, d), lambda i, be, nu, fl: (l, be[i], 0, 0)),
        ],
        out_specs=pl.BlockSpec((bm, d), lambda i, be, nu, fl: (i, 0)),
        scratch_shapes=[pltpu.VMEM((fh, d), BF16), pltpu.VMEM((fh, d), BF16)],
    )
    return pl.pallas_call(
        _expert_down_body,
        grid_spec=grid_spec,
        out_shape=jax.ShapeDtypeStruct((nb * bm, d), F32),
        compiler_params=_params(("arbitrary",), 56),
        name="moe_expert_down",
    )(block_e, n_used, first, act0, act1, w_down, w_down, b_down)


def _combine_body(pos_ref, nxt_ref, y_hbm, p_ref, x_ref, mod_ref, o_ref, buf_ref, sem_ref, *, nblk):
    i = pl.program_id(0)
    slot = lax.rem(i, 2)

    def start(idx_ref, s):
        for k in range(TOP_K):
            _start_row_gather(idx_ref, k, y_hbm, buf_ref.at[s, k], sem_ref.at[s], ROW_BLOCK)

    @pl.when(i == 0)
    def _():
        start(pos_ref, 0)

    @pl.when(i + 1 < nblk)
    def _():
        start(nxt_ref, 1 - slot)

    for k in range(TOP_K):
        _wait_row_gather(y_hbm, buf_ref.at[slot, k], sem_ref.at[slot], ROW_BLOCK)

    p = p_ref[...]
    acc = p[:, 0:1] * buf_ref[slot, 0]
    for k in range(1, TOP_K):
        acc = acc + p[:, k:k + 1] * buf_ref[slot, k]
    o_ref[...] = x_ref[...] + mod_ref[5:6, :] * acc


def _combine_call(lay, l, pos, y_sorted, top_p, x, mod):
    t, d = x.shape
    nblk = lay.nblk
    body = functools.partial(_combine_body, nblk=nblk)
    return pl.pallas_call(
        body,
        grid=(nblk,),
        in_specs=[
            pl.BlockSpec((1, TOP_K, ROW_BLOCK), lambda i: (i, 0, 0), memory_space=pltpu.SMEM),
            pl.BlockSpec((1, TOP_K, ROW_BLOCK), lambda i: (jnp.minimum(i + 1, nblk - 1), 0, 0),
                         memory_space=pltpu.SMEM),
            pl.BlockSpec(memory_space=pl.ANY),
            pl.BlockSpec((ROW_BLOCK, LANES), lambda i: (i, 0)),
            pl.BlockSpec((ROW_BLOCK, d), lambda i: (i, 0)),
            pl.BlockSpec((None, None, 6, d), lambda i: (l, lay.cond_row(i, ROW_BLOCK), 0, 0)),
        ],
        out_specs=pl.BlockSpec((ROW_BLOCK, d), lambda i: (i, 0)),
        out_shape=jax.ShapeDtypeStruct((t, d), F32),
        scratch_shapes=[pltpu.VMEM((2, TOP_K, ROW_BLOCK, d), F32), pltpu.SemaphoreType.DMA((2,))],
        compiler_params=_params(("arbitrary",), 40),
        name="moe_combine",
    )(pos, pos, y_sorted, top_p, x, mod)


def _moe(lay, l, x, norm_w, mod, w_router, b_router, w_up, b_up, w_down, b_down, bm):
    t = x.shape[0]
    n_experts = w_router.shape[-1]
    h2, top_e, top_p, rank, counts = _router_call(lay, l, x, norm_w, mod, w_router, b_router)
    counts = counts[0]
    padded = (counts + bm - 1) // bm * bm
    pad_end = jnp.cumsum(padded)
    pad_start = pad_end - padded
    e4 = top_e[:, :TOP_K]
    pos = pad_start[e4] + rank[:, :TOP_K]
    nb = t * TOP_K // bm + n_experts
    tok = jnp.broadcast_to(jnp.arange(t, dtype=jnp.int32)[:, None], (t, TOP_K))
    row_tok = jnp.zeros((nb * bm,), jnp.int32).at[pos.reshape(-1)].set(tok.reshape(-1))
    block_start = jnp.arange(nb, dtype=jnp.int32)[:, None] * bm
    block_e = jnp.minimum(jnp.sum((pad_end[None, :] <= block_start).astype(jnp.int32), axis=1), n_experts - 1)
    n_used = (pad_end[-1] // bm).astype(jnp.int32).reshape(1)
    first = jnp.concatenate([jnp.ones((1,), jnp.int32), (block_e[1:] != block_e[:-1]).astype(jnp.int32)])
    act0, xs = _expert_up_call(l, block_e, n_used, first, 0, (row_tok.reshape(nb, 1, bm), h2), w_up, b_up, bm)
    act1 = _expert_up_call(l, block_e, n_used, first, 1, xs, w_up, b_up, bm)
    y_sorted = _expert_down_call(l, block_e, n_used, first, act0, act1, w_down, b_down, bm)
    pos3 = pos.reshape(lay.nblk, ROW_BLOCK, TOP_K).transpose(0, 2, 1)
    return _combine_call(lay, l, pos3, y_sorted, top_p, x, mod)


def _final_norm_body(x_ref, w_ref, o_ref):
    x = x_ref[...]
    o_ref[...] = x * lax.rsqrt(jnp.mean(x * x, axis=-1, keepdims=True) + NORM_EPS) * w_ref[...]


def _final_norm_call(x, w):
    t, d = x.shape
    tm = 512
    return pl.pallas_call(
        _final_norm_body,
        grid=(t // tm,),
        in_specs=[pl.BlockSpec((tm, d), lambda i: (i, 0)), pl.BlockSpec((1, d), lambda i: (0, 0))],
        out_specs=pl.BlockSpec((tm, d), lambda i: (i, 0)),
        out_shape=jax.ShapeDtypeStruct((t, d), F32),
        compiler_params=_params(("arbitrary",), 32),
        name="final_norm",
    )(x, w.reshape(1, d))


def _grid_pos(n_tok, d_model):
    rows = n_tok // GRID_W
    r = jnp.repeat(jnp.arange(rows, dtype=F32), GRID_W)
    col = jnp.tile(jnp.arange(GRID_W, dtype=F32), rows)
    quarter = d_model // 4
    omega = 1.0 / (10000.0 ** (jnp.arange(quarter, dtype=F32) / quarter))
    ang_r = r[:, None] * omega
    ang_c = col[:, None] * omega
    return jnp.concatenate([jnp.sin(ang_r), jnp.cos(ang_r), jnp.sin(ang_c), jnp.cos(ang_c)], axis=-1)


def _s5_discretise(lam_re, lam_im, log_step, b_re, b_im):
    step = jnp.exp(log_step)[..., None]
    ang = lam_im * step
    mag = jnp.exp(lam_re * step)
    abar_re = mag * jnp.cos(ang)
    abar_im = mag * jnp.sin(ang)
    num_re = abar_re - 1.0
    den = lam_re * lam_re + lam_im * lam_im
    coef_re = (num_re * lam_re + abar_im * lam_im) / den
    coef_im = (abar_im * lam_re - num_re * lam_im) / den
    bb_re = coef_re[..., None] * b_re[:, None] - coef_im[..., None] * b_im[:, None]
    bb_im = coef_re[..., None] * b_im[:, None] + coef_im[..., None] * b_re[:, None]
    return abar_re, abar_im, bb_re, bb_im


def _block_diag_chunks(m, groups_per_chunk):
    lead = m.shape[:-3]
    g, a, b = m.shape[-3:]
    nch = g // groups_per_chunk
    m = m.reshape(lead + (nch, groups_per_chunk, a, b))
    eye = jnp.eye(groups_per_chunk, dtype=m.dtype)
    out = jnp.einsum("...cgab,gh->...cgahb", m, eye)
    return out.reshape(lead + (nch, groups_per_chunk * a, groups_per_chunk * b))


def kernel(x_prompt, x_sample, state_ssd, state_s5, state_delta, c, c_ctx, norm1, norm2, w_ada, b_ada, w_in,
           ssd_conv_w, ssd_conv_b, ssd_a_log, ssd_dt_bias, ssd_d, ssd_norm, s5_lam_re, s5_lam_im, s5_log_step,
           s5_b_re, s5_b_im, s5_c_re, s5_c_im, s5_d, s5_w_glu, dn_conv_w, dn_a_log, dn_dt_bias, dn_norm,
           w_branch, w_out, w_router, b_router, w_up, b_up, w_down, b_down, final_norm):
    n_ctx, l_ctx, d = x_prompt.shape
    n_smp, l_smp, _ = x_sample.shape
    depth = norm1.shape[0]
    w = d // 2
    lay = _Layout(n_ctx, l_ctx, n_smp, l_smp)
    t = lay.t

    h_ssd = ssd_a_log.shape[-1]
    n_state = state_ssd.shape[-1]
    xbc_w = ssd_conv_w.shape[-1]
    bc_w = xbc_w - w
    gs, ps = s5_lam_re.shape[-2:]
    js = w // gs
    h_dn = dn_a_log.shape[-1]
    n_experts = w_router.shape[-1]
    assert w // h_ssd == CHUNK and n_state == LANES and w // h_dn == LANES and w % 512 == 0
    assert LANES % js == 0 and l_ctx % ROW_BLOCK == 0 and l_smp % ROW_BLOCK == 0
    assert n_ctx % SUBLANES == 0 and n_smp % SUBLANES == 0 and 2 * h_ssd + 4 * h_dn <= LANES
    assert n_smp + 1 <= 2 * SUBLANES

    o_z, o_xbc, o_dt = 0, w, w + xbc_w
    o_u = o_dt + 2 * h_ssd
    o_qkv = o_u + w
    o_g = o_qkv + 3 * w
    o_a = o_g + w
    o_b = o_a + 2 * h_dn
    o_gate = o_b + 2 * h_dn
    seg = lambda a, n: w_in[:, :, a:a + n]
    n_small = 2 * h_ssd + 4 * h_dn
    col_gate, col_z, col_u, col_g, col_qkv = 0, 3 * d, 3 * d + w, 3 * d + 2 * w, 3 * d + 3 * w
    col_small = col_qkv + 3 * w + xbc_w
    n_cols = col_small + LANES
    n_pad = -n_cols % 1024
    w_in_r = jnp.concatenate(
        [seg(o_gate, 3 * d), seg(o_z, w), seg(o_u, w), seg(o_g, w), seg(o_qkv, 3 * w), seg(o_xbc, xbc_w),
         seg(o_dt, 2 * h_ssd), seg(o_a, 2 * h_dn), seg(o_b, 2 * h_dn),
         jnp.zeros((depth, d, LANES - n_small + n_pad), F32)], axis=-1).astype(BF16)
    conv_w = jnp.concatenate([dn_conv_w, ssd_conv_w], axis=-1)
    conv_b = jnp.concatenate([jnp.zeros((depth, 3 * w), F32), ssd_conv_b], axis=-1)[:, None, :]
    conv_width = 3 * w + xbc_w
    small_bias = jnp.concatenate([ssd_dt_bias.reshape(depth, -1), dn_dt_bias.reshape(depth, -1),
                                  jnp.zeros((depth, LANES - 2 * h_ssd - 2 * h_dn), F32)], axis=-1)[:, None, :]

    ssd_aneg = jnp.repeat(-jnp.exp(ssd_a_log), CHUNK, axis=-1)[:, :, None, :]
    ssd_d_exp = jnp.repeat(ssd_d, CHUNK, axis=-1)[:, None, :]
    dn_nega = jnp.concatenate([jnp.zeros((depth, 2 * h_ssd), F32), -jnp.exp(dn_a_log).reshape(depth, -1),
                               jnp.zeros((depth, LANES - 2 * h_ssd - 2 * h_dn), F32)], axis=-1)[:, None, :]
    dn_norm_t = jnp.tile(dn_norm, (1, h_dn))[:, None, :]
    e_dt = _expand_matrix(lambda dd, h: dd * h_ssd + h, h_ssd, CHUNK)
    e_g = _expand_matrix(lambda dd, h: 2 * h_ssd + dd * h_dn + h, h_dn, LANES)
    e_gc = _expand_matrix(lambda dd, h: 2 * h_ssd + dd * h_dn + h, h_dn, CHUNK)
    e_beta = _expand_matrix(lambda dd, h: 2 * h_ssd + 2 * h_dn + dd * h_dn + h, h_dn, LANES)
    ssd_consts = _chunk_consts(w)
    dn_consts = _chunk_consts(h_dn * CHUNK)

    gpc = LANES // js
    abar_re, abar_im, bb_re, bb_im = _s5_discretise(s5_lam_re, s5_lam_im, s5_log_step, s5_b_re, s5_b_im)
    n_s5 = gs * ps
    abar_re = abar_re.reshape(depth, 2, 1, n_s5)
    abar_im = abar_im.reshape(depth, 2, 1, n_s5)
    bb_re = _block_diag_chunks(jnp.swapaxes(bb_re, -1, -2), gpc).astype(BF16)
    bb_im = _block_diag_chunks(jnp.swapaxes(bb_im, -1, -2), gpc).astype(BF16)
    cc_re = _block_diag_chunks(jnp.swapaxes(s5_c_re, -1, -2), gpc).astype(BF16)
    cc_im = _block_diag_chunks(jnp.swapaxes(s5_c_im, -1, -2), gpc).astype(BF16)

    ssd_state_t = jnp.swapaxes(state_ssd.reshape(n_smp, depth, 2, w, n_state), -1, -2)
    s5_state = state_s5.reshape(n_smp // SUBLANES, SUBLANES, depth, 2, 2, n_s5).transpose(0, 2, 3, 4, 1, 5)

    w_glu_b = s5_w_glu.astype(BF16)
    w_branch_b = w_branch.astype(BF16)
    w_out_b = w_out.astype(BF16)
    b_up4 = b_up[:, :, None, :]
    b_down4 = b_down[:, :, None, :]
    norm1_3 = norm1[:, None, :]
    norm2_3 = norm2[:, None, :]
    b_router3 = b_router[:, None, :]
    ssd_norm3 = ssd_norm[:, None, :]
    s5_d3 = s5_d[:, None, :]

    cond = jnp.zeros((2 * SUBLANES, d), F32).at[0].set(c_ctx).at[1:1 + n_smp].set(c)
    mod = _ada_call(cond, w_ada, b_ada).reshape(depth, 2 * SUBLANES, 6, d)

    xs = x_sample + _grid_pos(l_smp, d)[None]
    x = jnp.concatenate([x_prompt.reshape(lay.t_ctx, d), xs.reshape(n_smp * l_smp, d)], axis=0)

    def to_time_major(a):
        cw = a.shape[-1]
        ctx = a[:lay.t_ctx].reshape(n_ctx // SUBLANES, SUBLANES, l_ctx, cw).transpose(0, 2, 1, 3)
        smp = a[lay.t_ctx:].reshape(n_smp // SUBLANES, SUBLANES, l_smp, cw).transpose(0, 2, 1, 3)
        return jnp.concatenate([ctx.reshape(lay.t_ctx, cw), smp.reshape(t - lay.t_ctx, cw)], axis=0)

    def from_time_major(a):
        cw = a.shape[-1]
        lead = a.shape[:-2]
        ctx = a[..., :lay.t_ctx, :].reshape(lead + (n_ctx // SUBLANES, l_ctx, SUBLANES, cw))
        smp = a[..., lay.t_ctx:, :].reshape(lead + (n_smp // SUBLANES, l_smp, SUBLANES, cw))
        ctx = jnp.swapaxes(ctx, -2, -3).reshape(lead + (lay.t_ctx, cw))
        smp = jnp.swapaxes(smp, -2, -3).reshape(lead + (t - lay.t_ctx, cw))
        return jnp.concatenate([ctx, smp], axis=-2)

    bm = 256
    ssd_out, s5_out, dn_out = [], [], []
    for l in range(depth):
        proj = _in_proj_call(lay, l, x, norm1_3, mod, w_in_r)
        conv = _conv_call(lay, l, proj, conv_w, conv_b, col_qkv, conv_width, 2 * w)
        y_ssd, ssd_fin = _ssd_call(lay, l, conv, proj, col_small, 3 * w, 4 * w, ssd_consts, e_dt, small_bias,
                                   ssd_aneg, ssd_state_t)
        u_tm = to_time_major(proj[:, col_u:col_u + w].astype(BF16))
        y_s5_tm, s5_fin = _s5_call(lay, l, u_tm, bb_re, bb_im, cc_re, cc_im, abar_re, abar_im, s5_state)
        y_s5 = from_time_major(y_s5_tm)
        o_dn, dn_fin = _dn_call(lay, l, conv, proj, col_small, dn_consts, e_beta, e_g, e_gc, small_bias,
                                dn_nega, state_delta)
        ys = _branch_call(lay, l, y_ssd, conv, 3 * w, proj, col_z, col_u, col_g, y_s5, o_dn,
                          ssd_d_exp, ssd_norm3, s5_d3, dn_norm_t, w_glu_b)
        merged = _merge_call(lay, l, ys, proj, col_gate, w_branch_b)
        x = _out_proj_call(lay, l, merged, w_out_b, x, mod)
        x = _moe(lay, l, x, norm2_3, mod, w_router, b_router3, w_up, b_up4, w_down, b_down4, bm)

        ssd_out.append(jnp.swapaxes(ssd_fin[:n_ctx], -1, -2).reshape(n_ctx, 2, h_ssd, CHUNK, n_state))
        ng = n_ctx // SUBLANES
        s5_out.append(s5_fin[:ng].transpose(0, 3, 1, 2, 4).reshape(n_ctx, 2, 2, gs, ps))
        dn_out.append(dn_fin[:n_ctx])

    y = _final_norm_call(x, final_norm)
    y_prompt = y[:lay.t_ctx].reshape(n_ctx, l_ctx, d)
    y_sample = y[lay.t_ctx:].reshape(n_smp, l_smp, d)
    return (y_prompt, y_sample, jnp.stack(ssd_out, axis=1), jnp.stack(s5_out, axis=1), jnp.stack(dn_out, axis=1))
```

```python
import functools

import numpy as np
import jax
import jax.numpy as jnp
from jax import lax
from jax.experimental import pallas as pl
from jax.experimental.pallas import tpu as pltpu

F32 = jnp.float32
BF16 = jnp.bfloat16
HIGHEST = lax.Precision.HIGHEST

ROW_BLOCK = 256
CHUNK = 64
LANES = 128
SUBLANES = 8
TOP_K = 4
NORM_EPS = 1e-6
L2_EPS = 1e-6
SWIGLU_ALPHA = 1.702
SWIGLU_LIMIT = 7.0
GRID_W = 64
NEG_BIG = -1e30
TILE_HEADS = 4
MIB = 1024 * 1024


def _params(sem, vmem_mib):
    return pltpu.CompilerParams(dimension_semantics=sem, vmem_limit_bytes=vmem_mib * MIB)


def _dot(a, b, precision=None):
    return jnp.dot(a, b, preferred_element_type=F32, precision=precision)


def _dot_nt(a, b):
    return lax.dot_general(a, b, (((1,), (1,)), ((), ())), preferred_element_type=F32)


def _dot_tn(a, b):
    return lax.dot_general(a, b, (((0,), (0,)), ((), ())), preferred_element_type=F32)


def _split_bf16(x, parts):
    out = []
    for n in range(parts):
        piece = x.astype(BF16)
        out.append(piece)
        if n + 1 < parts:
            x = x - piece.astype(F32)
    return out


def _dot_01_left(m01, x, parts=3):
    pieces = _split_bf16(x, parts)
    acc = _dot(m01, pieces[0])
    for piece in pieces[1:]:
        acc = acc + _dot(m01, piece)
    return acc


def _dot_01_right(x, m01, parts=3):
    pieces = _split_bf16(x, parts)
    acc = _dot(pieces[0], m01)
    for piece in pieces[1:]:
        acc = acc + _dot(piece, m01)
    return acc


def _sigmoid(x):
    return 1.0 / (1.0 + jnp.exp(-x))


def _silu(x):
    return x * _sigmoid(x)


def _softplus(x):
    return jnp.maximum(x, 0.0) + jnp.log1p(jnp.exp(-jnp.abs(x)))


class _Layout:
    def __init__(self, n_ctx, l_ctx, n_smp, l_smp):
        self.n_ctx, self.l_ctx, self.n_smp, self.l_smp = n_ctx, l_ctx, n_smp, l_smp
        self.t_ctx = n_ctx * l_ctx
        self.t = self.t_ctx + n_smp * l_smp
        self.nblk = self.t // ROW_BLOCK
        self.nblk_ctx = self.t_ctx // ROW_BLOCK
        self.bps_ctx = l_ctx // ROW_BLOCK
        self.bps_smp = l_smp // ROW_BLOCK
        self.nseq = n_ctx + n_smp

    def flags(self, ib):
        is_ctx = ib < self.nblk_ctx
        j = jnp.where(is_ctx, ib, ib - self.nblk_ctx)
        bps = jnp.where(is_ctx, self.bps_ctx, self.bps_smp)
        pos = lax.rem(j, bps)
        seq = jnp.where(is_ctx, lax.div(j, bps), self.n_ctx + lax.div(j, bps))
        return is_ctx, pos == 0, pos == bps - 1, seq

    def smp_seq(self, ib):
        j = jnp.maximum(ib - self.nblk_ctx, 0)
        return lax.div(j, self.bps_smp)

    def cond_row(self, i, rows):
        n_ctx_tiles = self.t_ctx // rows
        return jnp.where(i < n_ctx_tiles, 0, 1 + lax.div(jnp.maximum(i - n_ctx_tiles, 0), self.l_smp // rows))


def _ada_body(c_ref, w_ref, b_ref, o_ref):
    c = c_ref[...]
    o_ref[...] = _dot(_silu(c).astype(BF16), w_ref[...].astype(BF16)) + b_ref[...]


def _ada_call(cond, w_ada, b_ada):
    depth, d, n = w_ada.shape
    rows = cond.shape[0]
    tn = 1024
    return pl.pallas_call(
        _ada_body,
        grid=(depth, n // tn),
        in_specs=[
            pl.BlockSpec((rows, d), lambda l, j: (0, 0)),
            pl.BlockSpec((None, d, tn), lambda l, j: (l, 0, j)),
            pl.BlockSpec((None, 1, tn), lambda l, j: (l, 0, j)),
        ],
        out_specs=pl.BlockSpec((None, rows, tn), lambda l, j: (l, 0, j)),
        out_shape=jax.ShapeDtypeStruct((depth, rows, n), F32),
        compiler_params=_params(("arbitrary", "arbitrary"), 40),
        name="ada_mod",
    )(cond, w_ada, b_ada.reshape(depth, 1, n))


def _modulated_norm(x, nw, shift, scale):
    ms = jnp.mean(x * x, axis=-1, keepdims=True)
    y = x * lax.rsqrt(ms + NORM_EPS) * nw
    return y * (1.0 + scale) + shift


def _in_proj_body(x_ref, nw_ref, mod_ref, w_ref, o_ref, h_ref):
    @pl.when(pl.program_id(1) == 0)
    def _():
        h = _modulated_norm(x_ref[...], nw_ref[...], mod_ref[0:1, :], mod_ref[1:2, :])
        h_ref[...] = h.astype(BF16)

    o_ref[...] = _dot(h_ref[...], w_ref[...])


def _in_proj_call(lay, l, x, norm_w, mod, w_in):
    t, d = x.shape
    n = w_in.shape[-1]
    tm = min(1024, lay.l_smp)
    tn = 1024
    return pl.pallas_call(
        _in_proj_body,
        grid=(t // tm, n // tn),
        in_specs=[
            pl.BlockSpec((tm, d), lambda i, j: (i, 0)),
            pl.BlockSpec((None, 1, d), lambda i, j: (l, 0, 0)),
            pl.BlockSpec((None, None, 6, d), lambda i, j: (l, lay.cond_row(i, tm), 0, 0)),
            pl.BlockSpec((None, d, tn), lambda i, j: (l, 0, j)),
        ],
        out_specs=pl.BlockSpec((tm, tn), lambda i, j: (i, j)),
        out_shape=jax.ShapeDtypeStruct((t, n), F32),
        scratch_shapes=[pltpu.VMEM((tm, d), BF16)],
        compiler_params=_params(("arbitrary", "arbitrary"), 48),
        name="in_proj",
    )(x, norm_w, mod, w_in)


def _conv_body(x_ref, p_ref, n_ref, w_ref, b_ref, o_ref, ext_ref, *, lay, n_col_blocks, qk_width, conv_k):
    i = pl.program_id(0)
    c = pl.program_id(1)
    _, first, last, _ = lay.flags(i)
    halo = SUBLANES
    cb = x_ref.shape[1]
    ext_ref[0:halo, :] = jnp.where(first, 0.0, p_ref[...])
    ext_ref[halo:halo + ROW_BLOCK, :] = x_ref[...]
    ext_ref[halo + ROW_BLOCK:2 * halo + ROW_BLOCK, :] = jnp.where(last, 0.0, n_ref[...])

    def head_slice(hs, normalise):
        sl = slice(hs * LANES, (hs + 1) * LANES)
        acc = b_ref[:, sl] + jnp.zeros((ROW_BLOCK, LANES), F32)
        for k in range(conv_k):
            off = halo - conv_k // 2 + k
            acc = acc + w_ref[k:k + 1, sl] * ext_ref[off:off + ROW_BLOCK, sl]
        y = _silu(acc)
        if normalise:
            y = y * lax.rsqrt(jnp.sum(y * y, axis=-1, keepdims=True) + L2_EPS)
        o_ref[:, sl] = y

    for cv in range(n_col_blocks):
        @pl.when(c == cv)
        def _(cv=cv):
            for hs in range(cb // LANES):
                head_slice(hs, cv * cb + hs * LANES < qk_width)


def _conv_call(lay, l, proj, conv_w, conv_b, col0, width, qk_width):
    t = proj.shape[0]
    cb = 1536 if (width % 1536 == 0 and col0 % 1536 == 0) else 512
    conv_k = conv_w.shape[1]
    off = col0 // cb
    rb8 = ROW_BLOCK // SUBLANES
    body = functools.partial(_conv_body, lay=lay, n_col_blocks=width // cb, qk_width=qk_width, conv_k=conv_k)
    return pl.pallas_call(
        body,
        grid=(lay.nblk, width // cb),
        in_specs=[
            pl.BlockSpec((ROW_BLOCK, cb), lambda i, c: (i, off + c)),
            pl.BlockSpec((SUBLANES, cb), lambda i, c: (jnp.maximum(i * rb8 - 1, 0), off + c)),
            pl.BlockSpec((SUBLANES, cb), lambda i, c: (jnp.minimum((i + 1) * rb8, t // SUBLANES - 1), off + c)),
            pl.BlockSpec((None, conv_k, cb), lambda i, c: (l, 0, c)),
            pl.BlockSpec((None, 1, cb), lambda i, c: (l, 0, c)),
        ],
        out_specs=pl.BlockSpec((ROW_BLOCK, cb), lambda i, c: (i, c)),
        out_shape=jax.ShapeDtypeStruct((t, width), F32),
        scratch_shapes=[pltpu.VMEM((ROW_BLOCK + 2 * SUBLANES, cb), F32)],
        compiler_params=_params(("arbitrary", "arbitrary"), 24),
        name="dwconv_silu",
    )(proj, proj, proj, conv_w, conv_b)


def _chunk_consts(width):
    r = np.arange(ROW_BLOCK)
    same = (r[:, None] // CHUNK) == (r[None, :] // CHUNK)
    tri = np.stack([same & (r[None, :] <= r[:, None]), same & (r[None, :] >= r[:, None])]).astype(np.float32)
    ones_bd = same.astype(np.float32)
    s = np.arange(width) % CHUNK
    rr = r % CHUNK
    mask = np.stack([rr[:, None] >= s[None, :], rr[:, None] <= s[None, :]]).astype(np.float32)
    eye = (rr[:, None] == s[None, :]).astype(np.float32)
    return jnp.asarray(tri, dtype=BF16), jnp.asarray(ones_bd, dtype=BF16), jnp.asarray(mask), jnp.asarray(eye)


def _expand_matrix(col_of_head, n_heads, lanes_per_head):
    e = np.zeros((2, LANES, n_heads * lanes_per_head), np.float32)
    for d in range(2):
        for h in range(n_heads):
            e[d, col_of_head(d, h), h * lanes_per_head:(h + 1) * lanes_per_head] = 1.0
    return jnp.asarray(e, dtype=BF16)


def _half_masks():
    lane = lax.broadcasted_iota(jnp.int32, (1, LANES), 1)
    m0 = (lane < CHUNK).astype(F32)
    return m0, 1.0 - m0


def _ssd_body(*refs, **static):
    d = pl.program_id(0)

    @pl.when(d == 0)
    def _():
        _ssd_direction(*refs, rev=False, **static)

    @pl.when(d == 1)
    def _():
        _ssd_direction(*refs, rev=True, **static)


def _ssd_direction(x_ref, bc_ref, sm_ref, e_ref, bias_ref, aneg_ref, tri_ref, ones_ref, mask_ref, eye_ref, s0_ref,
                   y_ref, sout_ref, st_ref, cs_ref, ld_ref, xd_ref, *, rev, lay, n_groups, n_state,
                   heads_per_group):
    i = pl.program_id(1)
    ib = lay.nblk - 1 - i if rev else i
    is_ctx, first, last, _ = lay.flags(ib)
    start, end = (last, first) if rev else (first, last)

    @pl.when(start & is_ctx)
    def _():
        st_ref[...] = jnp.zeros_like(st_ref)

    @pl.when(start & jnp.logical_not(is_ctx))
    def _():
        st_ref[...] = s0_ref[...]

    dt = _dot_01_right(_softplus(sm_ref[...] + bias_ref[...]), e_ref[...])
    a = dt * aneg_ref[...]
    cs = _dot_01_left(tri_ref[...], a)
    cs_cols = _dot_01_left(ones_ref[...], eye_ref[...] * cs)
    cs_ref[...] = cs
    ld_ref[...] = jnp.exp(jnp.where(mask_ref[...] > 0.5, cs - cs_cols, NEG_BIG))
    xd_ref[...] = x_ref[...] * dt

    m0, m1 = _half_masks()
    n_chunks = ROW_BLOCK // CHUNK
    gw = heads_per_group * CHUNK
    for k in range(n_chunks):
        c = n_chunks - 1 - k if rev else k
        rows = slice(c * CHUNK, (c + 1) * CHUNK)
        cs_c = cs_ref[rows, :]
        ld_c = ld_ref[rows, :]
        xd_c = xd_ref[rows, :]
        bc_c = bc_ref[rows, :]
        cs_tot = cs_c[0:1, :] if rev else cs_c[CHUNK - 1:CHUNK, :]
        ecs = jnp.exp(cs_c)
        etot = jnp.exp(cs_tot)
        xdec = xd_c * jnp.exp(cs_tot - cs_c)
        for g in range(n_groups):
            bg = bc_c[:, g * n_state:(g + 1) * n_state].astype(BF16)
            cg = bc_c[:, (n_groups + g) * n_state:(n_groups + g + 1) * n_state].astype(BF16)
            cb2 = _dot_nt(cg, jnp.concatenate([bg, bg], axis=0))
            lo = g * gw
            st_g = st_ref[:, lo:lo + gw]
            y_off = _dot(cg, st_g.astype(BF16)) * ecs[:, lo:lo + gw]
            y_diag = []
            for jj in range(heads_per_group // 2):
                sl = slice(lo + jj * LANES, lo + (jj + 1) * LANES)
                m = (cb2 * ld_c[:, sl]).astype(BF16)
                xj = xd_c[:, sl]
                bd = jnp.concatenate([xj * m0, xj * m1], axis=0).astype(BF16)
                y_diag.append(_dot(m, bd))
            y_ref[rows, lo:lo + gw] = jnp.concatenate(y_diag, axis=1) + y_off
            st_ref[:, lo:lo + gw] = etot[:, lo:lo + gw] * st_g + _dot_tn(bg, xdec[:, lo:lo + gw].astype(BF16))

    @pl.when(end)
    def _():
        sout_ref[...] = st_ref[...]


def _ssd_call(lay, l, conv, proj, small_col, x_col, bc_col, consts, e_dt, bias, aneg, state_t):
    t = conv.shape[0]
    w = aneg.shape[-1]
    n_state = state_t.shape[-2]
    bcw = conv.shape[1] - x_col - w
    n_groups = bcw // (2 * n_state)
    heads = w // CHUNK
    tri, ones_bd, mask, eye = consts
    rev = lambda d, i: jnp.where(d == 0, i, lay.nblk - 1 - i)
    body = functools.partial(_ssd_body, lay=lay, n_groups=n_groups, n_state=n_state,
                             heads_per_group=heads // n_groups)
    return pl.pallas_call(
        body,
        grid=(2, lay.nblk),
        in_specs=[
            pl.BlockSpec((ROW_BLOCK, w), lambda d, i: (rev(d, i), x_col // w)),
            pl.BlockSpec((ROW_BLOCK, bcw), lambda d, i: (rev(d, i), bc_col // bcw)),
            pl.BlockSpec((ROW_BLOCK, LANES), lambda d, i: (rev(d, i), small_col // LANES)),
            pl.BlockSpec((None, LANES, w), lambda d, i: (d, 0, 0)),
            pl.BlockSpec((None, 1, LANES), lambda d, i: (l, 0, 0)),
            pl.BlockSpec((None, None, 1, w), lambda d, i: (l, d, 0, 0)),
            pl.BlockSpec((None, ROW_BLOCK, ROW_BLOCK), lambda d, i: (d, 0, 0)),
            pl.BlockSpec((ROW_BLOCK, ROW_BLOCK), lambda d, i: (0, 0)),
            pl.BlockSpec((None, ROW_BLOCK, w), lambda d, i: (d, 0, 0)),
            pl.BlockSpec((ROW_BLOCK, w), lambda d, i: (0, 0)),
            pl.BlockSpec((None, None, None, n_state, w), lambda d, i: (lay.smp_seq(rev(d, i)), l, d, 0, 0)),
        ],
        out_specs=[
            pl.BlockSpec((None, ROW_BLOCK, w), lambda d, i: (d, rev(d, i), 0)),
            pl.BlockSpec((None, None, n_state, w), lambda d, i: (lay.flags(rev(d, i))[3], d, 0, 0)),
        ],
        out_shape=[
            jax.ShapeDtypeStruct((2, t, w), F32),
            jax.ShapeDtypeStruct((lay.nseq, 2, n_state, w), F32),
        ],
        scratch_shapes=[pltpu.VMEM((n_state, w), F32)] + [pltpu.VMEM((ROW_BLOCK, w), F32)] * 3,
        compiler_params=_params(("arbitrary", "arbitrary"), 40),
        name="ssd_scan",
    )(conv, conv, proj, e_dt, bias, aneg, tri, ones_bd, mask, eye, state_t)


def _s5_body(u_ref, bbr_ref, bbi_ref, cr_ref, ci_ref, ar_ref, ai_ref, h0_ref,
             y_ref, hout_ref, bur_ref, bui_ref, hr_ref, hi_ref, *, nblk, nblk_ctx, bpg_ctx, bpg_smp):
    d = pl.program_id(0)
    i = pl.program_id(1)
    ib = jnp.where(d == 0, i, nblk - 1 - i)
    is_ctx = ib < nblk_ctx
    j = jnp.where(is_ctx, ib, ib - nblk_ctx)
    bpg = jnp.where(is_ctx, bpg_ctx, bpg_smp)
    pos = lax.rem(j, bpg)
    start = jnp.where(d == 0, pos == 0, pos == bpg - 1)
    end = jnp.where(d == 0, pos == bpg - 1, pos == 0)

    @pl.when(start & is_ctx)
    def _():
        hr_ref[...] = jnp.zeros_like(hr_ref)
        hi_ref[...] = jnp.zeros_like(hi_ref)

    @pl.when(start & jnp.logical_not(is_ctx))
    def _():
        hr_ref[...] = h0_ref[0]
        hi_ref[...] = h0_ref[1]

    n_lane_chunks = u_ref.shape[1] // LANES
    sw = bbr_ref.shape[-1]
    for cc in range(n_lane_chunks):
        uc = u_ref[:, cc * LANES:(cc + 1) * LANES].astype(BF16)
        bur_ref[:, cc * sw:(cc + 1) * sw] = _dot(uc, bbr_ref[cc])
        bui_ref[:, cc * sw:(cc + 1) * sw] = _dot(uc, bbi_ref[cc])

    n_steps = ROW_BLOCK // SUBLANES
    n_state = bur_ref.shape[1]
    seg = 512

    def step(s, carry):
        tstep = jnp.where(d == 0, s, n_steps - 1 - s)
        rows = pl.ds(pl.multiple_of(tstep * SUBLANES, SUBLANES), SUBLANES)
        for q in range(n_state // seg):
            sl = slice(q * seg, (q + 1) * seg)
            hr = hr_ref[:, sl]
            hi = hi_ref[:, sl]
            ar = ar_ref[:, sl]
            ai = ai_ref[:, sl]
            nr = ar * hr - ai * hi + bur_ref[rows, sl]
            ni = ar * hi + ai * hr + bui_ref[rows, sl]
            hr_ref[:, sl] = nr
            hi_ref[:, sl] = ni
            bur_ref[rows, sl] = nr
            bui_ref[rows, sl] = ni
        return carry

    lax.fori_loop(0, n_steps, step, 0)

    for cc in range(n_lane_chunks):
        hr = bur_ref[:, cc * sw:(cc + 1) * sw].astype(BF16)
        hi = bui_ref[:, cc * sw:(cc + 1) * sw].astype(BF16)
        y_ref[:, cc * LANES:(cc + 1) * LANES] = _dot(hr, cr_ref[cc]) - _dot(hi, ci_ref[cc])

    @pl.when(end)
    def _():
        hout_ref[0] = hr_ref[...]
        hout_ref[1] = hi_ref[...]


def _s5_call(lay, l, u_tm, bb_re, bb_im, c_re, c_im, abar_re, abar_im, h0):
    t, w = u_tm.shape
    n_state = abar_re.shape[-1]
    nch = w // LANES
    sw = n_state // nch
    ng_ctx = lay.n_ctx // SUBLANES
    ng_smp = lay.n_smp // SUBLANES
    bpg_ctx = lay.l_ctx * SUBLANES // ROW_BLOCK
    bpg_smp = lay.l_smp * SUBLANES // ROW_BLOCK
    rev = lambda d, i: jnp.where(d == 0, i, lay.nblk - 1 - i)

    def group(ib):
        is_ctx = ib < lay.nblk_ctx
        return jnp.where(is_ctx, lax.div(ib, bpg_ctx), ng_ctx + lax.div(jnp.maximum(ib - lay.nblk_ctx, 0), bpg_smp))

    def smp_group(ib):
        return lax.div(jnp.maximum(ib - lay.nblk_ctx, 0), bpg_smp)

    body = functools.partial(_s5_body, nblk=lay.nblk, nblk_ctx=lay.nblk_ctx, bpg_ctx=bpg_ctx, bpg_smp=bpg_smp)
    return pl.pallas_call(
        body,
        grid=(2, lay.nblk),
        in_specs=[
            pl.BlockSpec((ROW_BLOCK, w), lambda d, i: (rev(d, i), 0)),
            pl.BlockSpec((None, None, nch, LANES, sw), lambda d, i: (l, d, 0, 0, 0)),
            pl.BlockSpec((None, None, nch, LANES, sw), lambda d, i: (l, d, 0, 0, 0)),
            pl.BlockSpec((None, nch, sw, LANES), lambda d, i: (l, 0, 0, 0)),
            pl.BlockSpec((None, nch, sw, LANES), lambda d, i: (l, 0, 0, 0)),
            pl.BlockSpec((None, None, SUBLANES, n_state), lambda d, i: (l, d, 0, 0)),
            pl.BlockSpec((None, None, SUBLANES, n_state), lambda d, i: (l, d, 0, 0)),
            pl.BlockSpec((None, None, None, 2, SUBLANES, n_state),
                         lambda d, i: (smp_group(rev(d, i)), l, d, 0, 0, 0)),
        ],
        out_specs=[
            pl.BlockSpec((None, ROW_BLOCK, w), lambda d, i: (d, rev(d, i), 0)),
            pl.BlockSpec((None, None, 2, SUBLANES, n_state), lambda d, i: (group(rev(d, i)), d, 0, 0, 0)),
        ],
        out_shape=[
            jax.ShapeDtypeStruct((2, t, w), F32),
            jax.ShapeDtypeStruct((ng_ctx + ng_smp, 2, 2, SUBLANES, n_state), F32),
        ],
        scratch_shapes=[pltpu.VMEM((ROW_BLOCK, n_state), F32)] * 2 + [pltpu.VMEM((SUBLANES, n_state), F32)] * 2,
        compiler_params=_params(("arbitrary", "arbitrary"), 40),
        name="s5_scan",
    )(u_tm, bb_re, bb_im, c_re, c_im, abar_re, abar_im, h0)


def _dn_body(*refs, **static):
    d = pl.program_id(0)

    @pl.when(d == 0)
    def _():
        _dn_direction(*refs, rev=False, **static)

    @pl.when(d == 1)
    def _():
        _dn_direction(*refs, rev=True, **static)


def _dn_direction(q_ref, k_ref, v_ref, sm_ref, eb_ref, eg_ref, egc_ref, bias_ref, nega_ref,
                  tri_ref, ones_ref, mask_ref, eye_ref, s0_ref,
                  o_ref, sout_ref, s_ref, u_ref, w_ref, at_ref, eg_s, gc_s, *, rev, lay, n_heads):
    i = pl.program_id(1)
    ib = lay.nblk - 1 - i if rev else i
    is_ctx, first, last, _ = lay.flags(ib)
    start, end = (last, first) if rev else (first, last)
    n_pairs = n_heads // 2
    n_quads = n_heads // TILE_HEADS
    n_chunks = ROW_BLOCK // CHUNK
    dk = LANES
    pw = 2 * LANES
    qw = TILE_HEADS * LANES
    qc = TILE_HEADS * CHUNK

    @pl.when(start)
    def _():
        s_ref[...] = jnp.zeros_like(s_ref)

    @pl.when(start & jnp.logical_not(is_ctx))
    def _():
        for p in range(n_pairs):
            s_ref[p, 0:dk, 0:dk] = s0_ref[2 * p]
            s_ref[p, dk:pw, dk:pw] = s0_ref[2 * p + 1]

    scale = dk ** -0.5
    sm = sm_ref[...]
    g_small = _softplus(sm + bias_ref[...]) * nega_ref[...]
    gc_small = _dot_01_left(tri_ref[...], g_small)
    beta = _dot_01_right(_sigmoid(sm), eb_ref[...], parts=2)
    gc_wide = _dot_01_right(gc_small, eg_ref[...])
    gc_cols = _dot_01_right(gc_small, egc_ref[...])
    gc_t = _dot_01_left(ones_ref[...], eye_ref[...] * gc_cols)
    incl = mask_ref[...]
    dec = jnp.exp(jnp.where(incl > 0.5, gc_cols - gc_t, NEG_BIG))
    strict = incl - eye_ref[...]
    eg = jnp.exp(gc_wide)
    eg_s[...] = eg
    gc_s[...] = gc_wide

    lane_w = lax.broadcasted_iota(jnp.int32, (1, qw), 1)
    head_lanes = [(lax.shift_right_logical(lane_w, 7) == h).astype(F32) for h in range(TILE_HEADS)]
    lane_c = lax.broadcasted_iota(jnp.int32, (1, qc), 1)
    head_cols = [(lax.shift_right_logical(lane_c, 6) == h).astype(F32) for h in range(TILE_HEADS)]
    ri = lax.broadcasted_iota(jnp.int32, (qc, qc), 0)
    ci = lax.broadcasted_iota(jnp.int32, (qc, qc), 1)
    same = lambda s: lax.shift_right_logical(ri, s) == lax.shift_right_logical(ci, s)
    ident = (ri == ci).astype(F32)
    diag16 = same(4).astype(F32)
    off32 = (same(5) & jnp.logical_not(same(4))).astype(F32)
    off64 = (same(6) & jnp.logical_not(same(5))).astype(F32)
    rp = lax.broadcasted_iota(jnp.int32, (pw, pw), 0)
    cp = lax.broadcasted_iota(jnp.int32, (pw, pw), 1)
    pair_diag = (lax.shift_right_logical(rp, 7) == lax.shift_right_logical(cp, 7)).astype(F32)

    def stack_heads(x, masks):
        return jnp.concatenate([x * m for m in masks], axis=0)

    scan_order = [n_chunks - 1 - k for k in range(n_chunks)] if rev else list(range(n_chunks))

    for k in scan_order:
        rs = slice(k * CHUNK, (k + 1) * CHUNK)
        for qd in range(n_quads):
            lw = slice(qd * qw, (qd + 1) * qw)
            lq = slice(qd * qc, (qd + 1) * qc)
            kk = k_ref[rs, lw]
            beta_q = beta[rs, lw]
            kb = kk * beta_q
            bdk = stack_heads(kk, head_lanes).astype(BF16)
            kq = jnp.concatenate([kb, q_ref[rs, lw] * scale], axis=0).astype(BF16)
            kq_k = _dot_nt(kq, bdk)
            kbk = kq_k[:CHUNK]
            dec_q = dec[rs, lq]
            a_bd = stack_heads(kbk * dec_q * strict[rs, lq], head_cols)
            p = -(a_bd * diag16)
            tinv = ident + p
            for _ in range(3):
                pb = p.astype(BF16)
                p = _dot(pb, pb)
                tinv = tinv + _dot(tinv.astype(BF16), p.astype(BF16))
            for off in (off32, off64):
                tb = tinv.astype(BF16)
                tinv = tinv - _dot(_dot(tb, (a_bd * off).astype(BF16)).astype(BF16), tb)
            vb = v_ref[rs, lw] * beta_q
            kbe = kb * eg[rs, lw]
            rhs = jnp.concatenate(
                [jnp.concatenate([vb[:, h * dk:(h + 1) * dk], kbe[:, h * dk:(h + 1) * dk]], axis=1) for h in range(TILE_HEADS)],
                axis=0)
            sol = _dot(tinv.astype(BF16), rhs.astype(BF16))
            u_ref[k, qd] = sol[:, :dk]
            w_ref[k, qd] = sol[:, dk:]
            at_ref[k, qd] = stack_heads(kq_k[CHUNK:] * dec_q, head_cols).astype(BF16)

    for c in scan_order:
        rows = slice(c * CHUNK, (c + 1) * CHUNK)
        gc_c = gc_s[rows, :]
        eg_c = eg_s[rows, :]
        gc_tot = gc_c[0:1, :] if rev else gc_c[CHUNK - 1:CHUNK, :]
        g_tot = jnp.exp(gc_tot)
        kdec = jnp.exp(gc_tot - gc_c)
        for qd in range(n_quads):
            u_q = u_ref[c, qd]
            w_q = w_ref[c, qd]
            vns, qss, vn_rows = [], [], []
            for pp in range(TILE_HEADS // 2):
                p = (TILE_HEADS // 2) * qd + pp
                lp = slice(p * pw, (p + 1) * pw)
                ra = slice(pp * LANES, pp * LANES + CHUNK)
                rb = slice(pp * LANES + CHUNK, (pp + 1) * LANES)
                wab = jnp.concatenate([w_q[ra], w_q[rb]], axis=1)
                qdec = q_ref[rows, lp] * scale * eg_c[:, lp]
                res = _dot(jnp.concatenate([wab, qdec], axis=0).astype(BF16), s_ref[p].astype(BF16))
                vn = jnp.concatenate([u_q[ra], u_q[rb]], axis=1) - res[:CHUNK]
                vns.append(vn)
                qss.append(res[CHUNK:])
                vn_rows += [vn[:, :dk], vn[:, dk:]]
            o_q = _dot(at_ref[c, qd], jnp.concatenate(vn_rows, axis=0).astype(BF16))
            for pp in range(TILE_HEADS // 2):
                p = (TILE_HEADS // 2) * qd + pp
                lp = slice(p * pw, (p + 1) * pw)
                ra = slice(pp * LANES, pp * LANES + CHUNK)
                rb = slice(pp * LANES + CHUNK, (pp + 1) * LANES)
                o_ref[rows, lp] = qss[pp] + jnp.concatenate([o_q[ra], o_q[rb]], axis=1)
                kd = (k_ref[rows, lp] * kdec[:, lp]).astype(BF16)
                s_ref[p] = s_ref[p] * g_tot[:, lp] + _dot_tn(kd, vns[pp].astype(BF16)) * pair_diag

    @pl.when(end)
    def _():
        for p in range(n_pairs):
            sout_ref[2 * p] = s_ref[p, 0:dk, 0:dk]
            sout_ref[2 * p + 1] = s_ref[p, dk:pw, dk:pw]


def _dn_call(lay, l, conv, proj, small_col, consts, e_beta, e_g, e_gc, bias, nega, state):
    t = conv.shape[0]
    w = e_beta.shape[-1]
    n_heads = w // LANES
    assert n_heads % TILE_HEADS == 0
    wc = n_heads * CHUNK
    tri, ones_bd, mask, eye = consts
    n_chunks = ROW_BLOCK // CHUNK
    rev = lambda d, i: jnp.where(d == 0, i, lay.nblk - 1 - i)
    body = functools.partial(_dn_body, lay=lay, n_heads=n_heads)
    return pl.pallas_call(
        body,
        grid=(2, lay.nblk),
        in_specs=[
            pl.BlockSpec((ROW_BLOCK, w), lambda d, i: (rev(d, i), 0)),
            pl.BlockSpec((ROW_BLOCK, w), lambda d, i: (rev(d, i), 1)),
            pl.BlockSpec((ROW_BLOCK, w), lambda d, i: (rev(d, i), 2)),
            pl.BlockSpec((ROW_BLOCK, LANES), lambda d, i: (rev(d, i), small_col // LANES)),
            pl.BlockSpec((None, LANES, w), lambda d, i: (d, 0, 0)),
            pl.BlockSpec((None, LANES, w), lambda d, i: (d, 0, 0)),
            pl.BlockSpec((None, LANES, wc), lambda d, i: (d, 0, 0)),
            pl.BlockSpec((None, 1, LANES), lambda d, i: (l, 0, 0)),
            pl.BlockSpec((None, 1, LANES), lambda d, i: (l, 0, 0)),
            pl.BlockSpec((None, ROW_BLOCK, ROW_BLOCK), lambda d, i: (d, 0, 0)),
            pl.BlockSpec((ROW_BLOCK, ROW_BLOCK), lambda d, i: (0, 0)),
            pl.BlockSpec((None, ROW_BLOCK, wc), lambda d, i: (d, 0, 0)),
            pl.BlockSpec((ROW_BLOCK, wc), lambda d, i: (0, 0)),
            pl.BlockSpec((None, None, None, n_heads, LANES, LANES),
                         lambda d, i: (lay.smp_seq(rev(d, i)), l, d, 0, 0, 0)),
        ],
        out_specs=[
            pl.BlockSpec((None, ROW_BLOCK, w), lambda d, i: (d, rev(d, i), 0)),
            pl.BlockSpec((None, None, n_heads, LANES, LANES), lambda d, i: (lay.flags(rev(d, i))[3], d, 0, 0, 0)),
        ],
        out_shape=[
            jax.ShapeDtypeStruct((2, t, w), F32),
            jax.ShapeDtypeStruct((lay.nseq, 2, n_heads, LANES, LANES), F32),
        ],
        scratch_shapes=[
            pltpu.VMEM((n_heads // 2, 2 * LANES, 2 * LANES), F32),
            pltpu.VMEM((n_chunks, n_heads // TILE_HEADS, TILE_HEADS * CHUNK, LANES), F32),
            pltpu.VMEM((n_chunks, n_heads // TILE_HEADS, TILE_HEADS * CHUNK, LANES), F32),
            pltpu.VMEM((n_chunks, n_heads // TILE_HEADS, TILE_HEADS * CHUNK, TILE_HEADS * CHUNK), BF16),
            pltpu.VMEM((ROW_BLOCK, w), F32),
            pltpu.VMEM((ROW_BLOCK, w), F32),
        ],
        compiler_params=_params(("arbitrary", "arbitrary"), 40),
        name="deltanet_scan",
    )(conv, conv, conv, proj, e_beta, e_g, e_gc, bias, nega, tri, ones_bd, mask, eye, state)


def _gelu_tanh(x):
    return 0.5 * x * (1.0 + jnp.tanh(np.sqrt(2.0 / np.pi).astype(np.float32) * (x + 0.044715 * x * x * x)))


def _branch_body(ys_ref, xc_ref, z_ref, y5_ref, u_ref, od_ref, g_ref, dexp_ref, nssd_ref, s5d_ref, ndn_ref,
                 wglu_ref, o_ref):
    w = z_ref.shape[1]
    ya = (ys_ref[0] + ys_ref[1] + dexp_ref[...] * xc_ref[...]) * _silu(z_ref[...])
    ya = ya * lax.rsqrt(jnp.mean(ya * ya, axis=-1, keepdims=True) + NORM_EPS) * nssd_ref[...]
    o_ref[0] = ya.astype(BF16)

    yb = _gelu_tanh(y5_ref[0] + y5_ref[1] + s5d_ref[...] * u_ref[...])
    glu = _dot(yb.astype(BF16), wglu_ref[...])
    o_ref[1] = (glu[:, :w] * _sigmoid(glu[:, w:])).astype(BF16)

    gate = _silu(g_ref[...])
    for h in range(w // LANES):
        sl = slice(h * LANES, (h + 1) * LANES)
        o = od_ref[0, :, sl] + od_ref[1, :, sl]
        o = o * lax.rsqrt(jnp.mean(o * o, axis=-1, keepdims=True) + NORM_EPS) * ndn_ref[:, sl]
        o_ref[2, :, sl] = (o * gate[:, sl]).astype(BF16)


def _branch_call(lay, l, y_ssd, conv, x_col, proj, z_col, u_col, g_col, y_s5, o_dn, d_exp, n_ssd, s5_d, n_dn, w_glu):
    t = conv.shape[0]
    w = d_exp.shape[-1]
    vec = pl.BlockSpec((None, 1, w), lambda i: (l, 0, 0))
    pair = pl.BlockSpec((2, ROW_BLOCK, w), lambda i: (0, i, 0))
    return pl.pallas_call(
        _branch_body,
        grid=(lay.nblk,),
        in_specs=[
            pair,
            pl.BlockSpec((ROW_BLOCK, w), lambda i: (i, x_col // w)),
            pl.BlockSpec((ROW_BLOCK, w), lambda i: (i, z_col // w)),
            pair,
            pl.BlockSpec((ROW_BLOCK, w), lambda i: (i, u_col // w)),
            pair,
            pl.BlockSpec((ROW_BLOCK, w), lambda i: (i, g_col // w)),
            vec, vec, vec, vec,
            pl.BlockSpec((None, w, 2 * w), lambda i: (l, 0, 0)),
        ],
        out_specs=pl.BlockSpec((3, ROW_BLOCK, w), lambda i: (0, i, 0)),
        out_shape=jax.ShapeDtypeStruct((3, t, w), BF16),
        compiler_params=_params(("arbitrary",), 48),
        name="branch_epilogue",
    )(y_ssd, conv, proj, y_s5, proj, o_dn, proj, d_exp, n_ssd, s5_d, n_dn, w_glu)


def _merge_body(ys_ref, g0_ref, g1_ref, g2_ref, wb_ref, o_ref):
    acc = _sigmoid(g0_ref[...]) * _dot(ys_ref[0], wb_ref[0])
    acc = acc + _sigmoid(g1_ref[...]) * _dot(ys_ref[1], wb_ref[1])
    acc = acc + _sigmoid(g2_ref[...]) * _dot(ys_ref[2], wb_ref[2])
    o_ref[...] = acc.astype(BF16)


def _merge_call(lay, l, ys, proj, gate_col, w_branch):
    _, t, w = ys.shape
    d = w_branch.shape[-1]
    g0 = gate_col // d
    gates = [pl.BlockSpec((ROW_BLOCK, d), lambda i, k=k: (i, g0 + k)) for k in range(3)]
    return pl.pallas_call(
        _merge_body,
        grid=(lay.nblk,),
        in_specs=[pl.BlockSpec((3, ROW_BLOCK, w), lambda i: (0, i, 0))] + gates
        + [pl.BlockSpec((None, 3, w, d), lambda i: (l, 0, 0, 0))],
        out_specs=pl.BlockSpec((ROW_BLOCK, d), lambda i: (i, 0)),
        out_shape=jax.ShapeDtypeStruct((t, d), BF16),
        compiler_params=_params(("arbitrary",), 52),
        name="branch_merge",
    )(ys, proj, proj, proj, w_branch)


def _out_proj_body(m_ref, w_ref, x_ref, mod_ref, o_ref):
    o_ref[...] = x_ref[...] + mod_ref[2:3, :] * _dot(m_ref[...], w_ref[...])


def _out_proj_call(lay, l, merged, w_out, x, mod):
    t, d = x.shape
    tm = min(512, lay.l_smp)
    return pl.pallas_call(
        _out_proj_body,
        grid=(t // tm,),
        in_specs=[
            pl.BlockSpec((tm, d), lambda i: (i, 0)),
            pl.BlockSpec((None, d, d), lambda i: (l, 0, 0)),
            pl.BlockSpec((tm, d), lambda i: (i, 0)),
            pl.BlockSpec((None, None, 6, d), lambda i: (l, lay.cond_row(i, tm), 0, 0)),
        ],
        out_specs=pl.BlockSpec((tm, d), lambda i: (i, 0)),
        out_shape=jax.ShapeDtypeStruct((t, d), F32),
        compiler_params=_params(("arbitrary",), 48),
        name="out_proj",
    )(merged, w_out, x, mod)


def _router_body(x_ref, nw_ref, mod_ref, wr_ref, br_ref, lt_ref,
                 h_ref, e_ref, p_ref, r_ref, cnt_ref, run_ref, *, n_experts):
    i = pl.program_id(0)

    @pl.when(i == 0)
    def _():
        run_ref[...] = jnp.zeros_like(run_ref)

    h = _modulated_norm(x_ref[...], nw_ref[...], mod_ref[3:4, :], mod_ref[4:5, :])
    h_ref[...] = h
    logits = _dot(h, wr_ref[...], HIGHEST) + br_ref[...]
    rows = logits.shape[0]
    lane = lax.broadcasted_iota(jnp.int32, (rows, n_experts), 1).astype(F32)
    out_lane = lax.broadcasted_iota(jnp.int32, (rows, LANES), 1)
    cur = logits
    sels, tops, idxs = [], [], []
    for _ in range(TOP_K):
        m = jnp.max(cur, axis=-1, keepdims=True)
        idx = jnp.min(jnp.where(cur == m, lane, float(n_experts)), axis=-1, keepdims=True)
        sel = lane == idx
        sels.append(sel)
        tops.append(m)
        idxs.append(idx)
        cur = jnp.where(sel, -jnp.inf, cur)
    exps = [jnp.exp(m - tops[0]) for m in tops]
    denom = exps[0] + exps[1] + exps[2] + exps[3]
    onehot = jnp.zeros((rows, n_experts), F32)
    for sel in sels:
        onehot = onehot + sel.astype(F32)
    before = _dot(lt_ref[...], onehot.astype(BF16)) + run_ref[...]
    e_out = jnp.zeros((rows, LANES), F32)
    p_out = jnp.zeros((rows, LANES), F32)
    r_out = jnp.zeros((rows, LANES), F32)
    for k in range(TOP_K):
        rank = jnp.sum(jnp.where(sels[k], before, 0.0), axis=-1, keepdims=True)
        e_out = jnp.where(out_lane == k, idxs[k], e_out)
        p_out = jnp.where(out_lane == k, exps[k] / denom, p_out)
        r_out = jnp.where(out_lane == k, rank, r_out)
    e_ref[...] = e_out.astype(jnp.int32)
    p_ref[...] = p_out
    r_ref[...] = r_out.astype(jnp.int32)
    run_ref[...] = run_ref[...] + jnp.sum(onehot, axis=0, keepdims=True)
    cnt_ref[...] = jnp.broadcast_to(run_ref[...], cnt_ref.shape).astype(jnp.int32)


def _router_call(lay, l, x, norm_w, mod, w_router, b_router):
    t, d = x.shape
    n_experts = w_router.shape[-1]
    r = np.arange(ROW_BLOCK)
    lower = jnp.asarray((r[None, :] < r[:, None]).astype(np.float32), dtype=BF16)
    wide = pl.BlockSpec((ROW_BLOCK, LANES), lambda i: (i, 0))
    body = functools.partial(_router_body, n_experts=n_experts)
    return pl.pallas_call(
        body,
        grid=(lay.nblk,),
        in_specs=[
            pl.BlockSpec((ROW_BLOCK, d), lambda i: (i, 0)),
            pl.BlockSpec((None, 1, d), lambda i: (l, 0, 0)),
            pl.BlockSpec((None, None, 6, d), lambda i: (l, lay.cond_row(i, ROW_BLOCK), 0, 0)),
            pl.BlockSpec((None, d, n_experts), lambda i: (l, 0, 0)),
            pl.BlockSpec((None, 1, n_experts), lambda i: (l, 0, 0)),
            pl.BlockSpec((ROW_BLOCK, ROW_BLOCK), lambda i: (0, 0)),
        ],
        out_specs=[
            pl.BlockSpec((ROW_BLOCK, d), lambda i: (i, 0)),
            wide, wide, wide,
            pl.BlockSpec((SUBLANES, n_experts), lambda i: (0, 0)),
        ],
        out_shape=[
            jax.ShapeDtypeStruct((t, d), F32),
            jax.ShapeDtypeStruct((t, LANES), jnp.int32),
            jax.ShapeDtypeStruct((t, LANES), F32),
            jax.ShapeDtypeStruct((t, LANES), jnp.int32),
            jax.ShapeDtypeStruct((SUBLANES, n_experts), jnp.int32),
        ],
        scratch_shapes=[pltpu.VMEM((1, n_experts), F32)],
        compiler_params=_params(("arbitrary",), 32),
        name="moe_router",
    )(x, norm_w, mod, w_router, b_router, lower)


GATHER_UNROLL = 8


def _row_copy(src_hbm, src_row, dst, dst_row, sem):
    return pltpu.make_async_copy(src_hbm.at[pl.ds(src_row, 1)], dst.at[pl.ds(dst_row, 1)], sem)


def _start_row_gather(idx_ref, k, src_hbm, dst, sem, n_rows):
    def body(g, carry):
        for u in range(GATHER_UNROLL):
            r = g * GATHER_UNROLL + u
            _row_copy(src_hbm, idx_ref[0, k, r], dst, r, sem).start(priority=u % 2)
        return carry

    lax.fori_loop(0, n_rows // GATHER_UNROLL, body, 0)


def _wait_row_gather(src_hbm, dst, sem, n_rows):
    pltpu.make_async_copy(src_hbm.at[pl.ds(0, n_rows)], dst, sem).wait()


def _cast_weights_on_expert_change(fl_ref, n_used, pairs):
    i = pl.program_id(0)

    @pl.when((fl_ref[i] == 1) & (i < n_used))
    def _():
        for w_ref, wb_ref in pairs:
            wb_ref[...] = w_ref[...].astype(BF16)


def _swiglu_half(x, wg_ref, wl_ref, bg_ref, bl_ref):
    glu = jnp.minimum(_dot(x, wg_ref[...]) + bg_ref[...], SWIGLU_LIMIT)
    lin = jnp.clip(_dot(x, wl_ref[...]) + bl_ref[...], -SWIGLU_LIMIT, SWIGLU_LIMIT)
    return (glu * _sigmoid(SWIGLU_ALPHA * glu) * (lin + 1.0)).astype(BF16)


def _expert_up_gather_body(be_ref, nu_ref, fl_ref, tok_ref, nxt_ref, h_hbm, wg_ref, wl_ref, bg_ref, bl_ref,
                           act_ref, xs_ref, wgb_ref, wlb_ref, buf_ref, sem_ref, *, bm):
    i = pl.program_id(0)
    n_used = nu_ref[0]
    slot = lax.rem(i, 2)

    @pl.when(i == 0)
    def _():
        _start_row_gather(tok_ref, 0, h_hbm, buf_ref.at[0], sem_ref.at[0], bm)

    @pl.when(i + 1 < n_used)
    def _():
        _start_row_gather(nxt_ref, 0, h_hbm, buf_ref.at[1 - slot], sem_ref.at[1 - slot], bm)

    @pl.when(i < jnp.maximum(n_used, 1))
    def _():
        _wait_row_gather(h_hbm, buf_ref.at[slot], sem_ref.at[slot], bm)

    _cast_weights_on_expert_change(fl_ref, n_used, ((wg_ref, wgb_ref), (wl_ref, wlb_ref)))

    @pl.when(i < n_used)
    def _():
        x = buf_ref[slot].astype(BF16)
        xs_ref[...] = x
        act_ref[...] = _swiglu_half(x, wgb_ref, wlb_ref, bg_ref, bl_ref)

    @pl.when(i >= n_used)
    def _():
        xs_ref[...] = jnp.zeros_like(xs_ref)
        act_ref[...] = jnp.zeros_like(act_ref)


def _expert_up_body(be_ref, nu_ref, fl_ref, xs_ref, wg_ref, wl_ref, bg_ref, bl_ref, act_ref, wgb_ref, wlb_ref):
    i = pl.program_id(0)
    n_used = nu_ref[0]
    _cast_weights_on_expert_change(fl_ref, n_used, ((wg_ref, wgb_ref), (wl_ref, wlb_ref)))

    @pl.when(i < n_used)
    def _():
        act_ref[...] = _swiglu_half(xs_ref[...], wgb_ref, wlb_ref, bg_ref, bl_ref)

    @pl.when(i >= n_used)
    def _():
        act_ref[...] = jnp.zeros_like(act_ref)


def _expert_up_call(l, block_e, n_used, first, half, rows, w_up, b_up, bm):
    nb = block_e.shape[0]
    d = w_up.shape[-2]
    d_ff = w_up.shape[-1] // 2
    n_half = 2
    fh = d_ff // n_half
    w_specs = [
        pl.BlockSpec((None, None, d, fh), lambda i, be, nu, fl: (l, be[i], 0, half)),
        pl.BlockSpec((None, None, d, fh), lambda i, be, nu, fl: (l, be[i], 0, n_half + half)),
        pl.BlockSpec((None, None, 1, fh), lambda i, be, nu, fl: (l, be[i], 0, half)),
        pl.BlockSpec((None, None, 1, fh), lambda i, be, nu, fl: (l, be[i], 0, n_half + half)),
    ]
    act_spec = pl.BlockSpec((bm, fh), lambda i, be, nu, fl: (i, 0))
    xs_spec = pl.BlockSpec((bm, d), lambda i, be, nu, fl: (i, 0))
    act_shape = jax.ShapeDtypeStruct((nb * bm, fh), BF16)
    w_scratch = [pltpu.VMEM((d, fh), BF16), pltpu.VMEM((d, fh), BF16)]
    if half == 0:
        row_tok, h2 = rows
        grid_spec = pltpu.PrefetchScalarGridSpec(
            num_scalar_prefetch=3,
            grid=(nb,),
            in_specs=[
                pl.BlockSpec((1, 1, bm), lambda i, be, nu, fl: (i, 0, 0), memory_space=pltpu.SMEM),
                pl.BlockSpec((1, 1, bm), lambda i, be, nu, fl: (jnp.minimum(i + 1, nb - 1), 0, 0),
                             memory_space=pltpu.SMEM),
                pl.BlockSpec(memory_space=pl.ANY),
            ] + w_specs,
            out_specs=[act_spec, xs_spec],
            scratch_shapes=w_scratch + [pltpu.VMEM((2, bm, d), F32), pltpu.SemaphoreType.DMA((2,))],
        )
        return pl.pallas_call(
            functools.partial(_expert_up_gather_body, bm=bm),
            grid_spec=grid_spec,
            out_shape=[act_shape, jax.ShapeDtypeStruct((nb * bm, d), BF16)],
            compiler_params=_params(("arbitrary",), 56),
            name="moe_expert_up_gather",
        )(block_e, n_used, first, row_tok, row_tok, h2, w_up, w_up, b_up, b_up)
    grid_spec = pltpu.PrefetchScalarGridSpec(
        num_scalar_prefetch=3,
        grid=(nb,),
        in_specs=[xs_spec] + w_specs,
        out_specs=act_spec,
        scratch_shapes=w_scratch,
    )
    return pl.pallas_call(
        _expert_up_body,
        grid_spec=grid_spec,
        out_shape=act_shape,
        compiler_params=_params(("arbitrary",), 56),
        name="moe_expert_up",
    )(block_e, n_used, first, rows, w_up, w_up, b_up, b_up)


def _expert_down_body(be_ref, nu_ref, fl_ref, a0_ref, a1_ref, w0_ref, w1_ref, b_ref, o_ref, wb0_ref, wb1_ref):
    i = pl.program_id(0)
    n_used = nu_ref[0]
    _cast_weights_on_expert_change(fl_ref, n_used, ((w0_ref, wb0_ref), (w1_ref, wb1_ref)))

    @pl.when(i < n_used)
    def _():
        o_ref[...] = _dot(a0_ref[...], wb0_ref[...]) + _dot(a1_ref[...], wb1_ref[...]) + b_ref[...]

    @pl.when(i >= n_used)
    def _():
        o_ref[...] = jnp.zeros_like(o_ref)


def _expert_down_call(l, block_e, n_used, first, act0, act1, w_down, b_down, bm):
    nb = block_e.shape[0]
    d_ff, d = w_down.shape[-2:]
    fh = act0.shape[1]
    act_spec = pl.BlockSpec((bm, fh), lambda i, be, nu, fl: (i, 0))
    grid_spec = pltpu.PrefetchScalarGridSpec(
        num_scalar_prefetch=3,
        grid=(nb,),
        in_specs=[
            act_spec, act_spec,
            pl.BlockSpec((None, None, fh, d), lambda i, be, nu, fl: (l, be[i], 0, 0)),
            pl.BlockSpec((None, None, fh, d), lambda i, be, nu, fl: (l, be[i], 1, 0)),
            pl.BlockSpec((None, None, 1, d), lambda i, be, nu, fl: (l, be[i], 0, 0)),
        ],
        out_specs=pl.BlockSpec((bm, d), lambda i, be, nu, fl: (i, 0)),
        scratch_shapes=[pltpu.VMEM((fh, d), BF16), pltpu.VMEM((fh, d), BF16)],
    )
    return pl.pallas_call(
        _expert_down_body,
        grid_spec=grid_spec,
        out_shape=jax.ShapeDtypeStruct((nb * bm, d), F32),
        compiler_params=_params(("arbitrary",), 56),
        name="moe_expert_down",
    )(block_e, n_used, first, act0, act1, w_down, w_down, b_down)


def _combine_body(pos_ref, nxt_ref, y_hbm, p_ref, x_ref, mod_ref, o_ref, buf_ref, sem_ref, *, nblk):
    i = pl.program_id(0)
    slot = lax.rem(i, 2)

    def start(idx_ref, s):
        for k in range(TOP_K):
            _start_row_gather(idx_ref, k, y_hbm, buf_ref.at[s, k], sem_ref.at[s], ROW_BLOCK)

    @pl.when(i == 0)
    def _():
        start(pos_ref, 0)

    @pl.when(i + 1 < nblk)
    def _():
        start(nxt_ref, 1 - slot)

    for k in range(TOP_K):
        _wait_row_gather(y_hbm, buf_ref.at[slot, k], sem_ref.at[slot], ROW_BLOCK)

    p = p_ref[...]
    acc = p[:, 0:1] * buf_ref[slot, 0]
    for k in range(1, TOP_K):
        acc = acc + p[:, k:k + 1] * buf_ref[slot, k]
    o_ref[...] = x_ref[...] + mod_ref[5:6, :] * acc


def _combine_call(lay, l, pos, y_sorted, top_p, x, mod):
    t, d = x.shape
    nblk = lay.nblk
    body = functools.partial(_combine_body, nblk=nblk)
    return pl.pallas_call(
        body,
        grid=(nblk,),
        in_specs=[
            pl.BlockSpec((1, TOP_K, ROW_BLOCK), lambda i: (i, 0, 0), memory_space=pltpu.SMEM),
            pl.BlockSpec((1, TOP_K, ROW_BLOCK), lambda i: (jnp.minimum(i + 1, nblk - 1), 0, 0),
                         memory_space=pltpu.SMEM),
            pl.BlockSpec(memory_space=pl.ANY),
            pl.BlockSpec((ROW_BLOCK, LANES), lambda i: (i, 0)),
            pl.BlockSpec((ROW_BLOCK, d), lambda i: (i, 0)),
            pl.BlockSpec((None, None, 6, d), lambda i: (l, lay.cond_row(i, ROW_BLOCK), 0, 0)),
        ],
        out_specs=pl.BlockSpec((ROW_BLOCK, d), lambda i: (i, 0)),
        out_shape=jax.ShapeDtypeStruct((t, d), F32),
        scratch_shapes=[pltpu.VMEM((2, TOP_K, ROW_BLOCK, d), F32), pltpu.SemaphoreType.DMA((2,))],
        compiler_params=_params(("arbitrary",), 40),
        name="moe_combine",
    )(pos, pos, y_sorted, top_p, x, mod)


def _moe(lay, l, x, norm_w, mod, w_router, b_router, w_up, b_up, w_down, b_down, bm):
    t = x.shape[0]
    n_experts = w_router.shape[-1]
    h2, top_e, top_p, rank, counts = _router_call(lay, l, x, norm_w, mod, w_router, b_router)
    counts = counts[0]
    padded = (counts + bm - 1) // bm * bm
    pad_end = jnp.cumsum(padded)
    pad_start = pad_end - padded
    e4 = top_e[:, :TOP_K]
    pos = pad_start[e4] + rank[:, :TOP_K]
    nb = t * TOP_K // bm + n_experts
    tok = jnp.broadcast_to(jnp.arange(t, dtype=jnp.int32)[:, None], (t, TOP_K))
    row_tok = jnp.zeros((nb * bm,), jnp.int32).at[pos.reshape(-1)].set(tok.reshape(-1))
    block_start = jnp.arange(nb, dtype=jnp.int32)[:, None] * bm
    block_e = jnp.minimum(jnp.sum((pad_end[None, :] <= block_start).astype(jnp.int32), axis=1), n_experts - 1)
    n_used = (pad_end[-1] // bm).astype(jnp.int32).reshape(1)
    first = jnp.concatenate([jnp.ones((1,), jnp.int32), (block_e[1:] != block_e[:-1]).astype(jnp.int32)])
    act0, xs = _expert_up_call(l, block_e, n_used, first, 0, (row_tok.reshape(nb, 1, bm), h2), w_up, b_up, bm)
    act1 = _expert_up_call(l, block_e, n_used, first, 1, xs, w_up, b_up, bm)
    y_sorted = _expert_down_call(l, block_e, n_used, first, act0, act1, w_down, b_down, bm)
    pos3 = pos.reshape(lay.nblk, ROW_BLOCK, TOP_K).transpose(0, 2, 1)
    return _combine_call(lay, l, pos3, y_sorted, top_p, x, mod)


def _final_norm_body(x_ref, w_ref, o_ref):
    x = x_ref[...]
    o_ref[...] = x * lax.rsqrt(jnp.mean(x * x, axis=-1, keepdims=True) + NORM_EPS) * w_ref[...]


def _final_norm_call(x, w):
    t, d = x.shape
    tm = 512
    return pl.pallas_call(
        _final_norm_body,
        grid=(t // tm,),
        in_specs=[pl.BlockSpec((tm, d), lambda i: (i, 0)), pl.BlockSpec((1, d), lambda i: (0, 0))],
        out_specs=pl.BlockSpec((tm, d), lambda i: (i, 0)),
        out_shape=jax.ShapeDtypeStruct((t, d), F32),
        compiler_params=_params(("arbitrary",), 32),
        name="final_norm",
    )(x, w.reshape(1, d))


def _grid_pos(n_tok, d_model):
    rows = n_tok // GRID_W
    r = jnp.repeat(jnp.arange(rows, dtype=F32), GRID_W)
    col = jnp.tile(jnp.arange(GRID_W, dtype=F32), rows)
    quarter = d_model // 4
    omega = 1.0 / (10000.0 ** (jnp.arange(quarter, dtype=F32) / quarter))
    ang_r = r[:, None] * omega
    ang_c = col[:, None] * omega
    return jnp.concatenate([jnp.sin(ang_r), jnp.cos(ang_r), jnp.sin(ang_c), jnp.cos(ang_c)], axis=-1)


def _s5_discretise(lam_re, lam_im, log_step, b_re, b_im):
    step = jnp.exp(log_step)[..., None]
    ang = lam_im * step
    mag = jnp.exp(lam_re * step)
    abar_re = mag * jnp.cos(ang)
    abar_im = mag * jnp.sin(ang)
    num_re = abar_re - 1.0
    den = lam_re * lam_re + lam_im * lam_im
    coef_re = (num_re * lam_re + abar_im * lam_im) / den
    coef_im = (abar_im * lam_re - num_re * lam_im) / den
    bb_re = coef_re[..., None] * b_re[:, None] - coef_im[..., None] * b_im[:, None]
    bb_im = coef_re[..., None] * b_im[:, None] + coef_im[..., None] * b_re[:, None]
    return abar_re, abar_im, bb_re, bb_im


def _block_diag_chunks(m, groups_per_chunk):
    lead = m.shape[:-3]
    g, a, b = m.shape[-3:]
    nch = g // groups_per_chunk
    m = m.reshape(lead + (nch, groups_per_chunk, a, b))
    eye = jnp.eye(groups_per_chunk, dtype=m.dtype)
    out = jnp.einsum("...cgab,gh->...cgahb", m, eye)
    return out.reshape(lead + (nch, groups_per_chunk * a, groups_per_chunk * b))


def kernel(x_prompt, x_sample, state_ssd, state_s5, state_delta, c, c_ctx, norm1, norm2, w_ada, b_ada, w_in,
           ssd_conv_w, ssd_conv_b, ssd_a_log, ssd_dt_bias, ssd_d, ssd_norm, s5_lam_re, s5_lam_im, s5_log_step,
           s5_b_re, s5_b_im, s5_c_re, s5_c_im, s5_d, s5_w_glu, dn_conv_w, dn_a_log, dn_dt_bias, dn_norm,
           w_branch, w_out, w_router, b_router, w_up, b_up, w_down, b_down, final_norm):
    n_ctx, l_ctx, d = x_prompt.shape
    n_smp, l_smp, _ = x_sample.shape
    depth = norm1.shape[0]
    w = d // 2
    lay = _Layout(n_ctx, l_ctx, n_smp, l_smp)
    t = lay.t

    h_ssd = ssd_a_log.shape[-1]
    n_state = state_ssd.shape[-1]
    xbc_w = ssd_conv_w.shape[-1]
    bc_w = xbc_w - w
    gs, ps = s5_lam_re.shape[-2:]
    js = w // gs
    h_dn = dn_a_log.shape[-1]
    n_experts = w_router.shape[-1]
    assert w // h_ssd == CHUNK and n_state == LANES and w // h_dn == LANES and w % 512 == 0
    assert LANES % js == 0 and l_ctx % ROW_BLOCK == 0 and l_smp % ROW_BLOCK == 0
    assert n_ctx % SUBLANES == 0 and n_smp % SUBLANES == 0 and 2 * h_ssd + 4 * h_dn <= LANES
    assert n_smp + 1 <= 2 * SUBLANES

    o_z, o_xbc, o_dt = 0, w, w + xbc_w
    o_u = o_dt + 2 * h_ssd
    o_qkv = o_u + w
    o_g = o_qkv + 3 * w
    o_a = o_g + w
    o_b = o_a + 2 * h_dn
    o_gate = o_b + 2 * h_dn
    seg = lambda a, n: w_in[:, :, a:a + n]
    n_small = 2 * h_ssd + 4 * h_dn
    col_gate, col_z, col_u, col_g, col_qkv = 0, 3 * d, 3 * d + w, 3 * d + 2 * w, 3 * d + 3 * w
    col_small = col_qkv + 3 * w + xbc_w
    n_cols = col_small + LANES
    n_pad = -n_cols % 1024
    w_in_r = jnp.concatenate(
        [seg(o_gate, 3 * d), seg(o_z, w), seg(o_u, w), seg(o_g, w), seg(o_qkv, 3 * w), seg(o_xbc, xbc_w),
         seg(o_dt, 2 * h_ssd), seg(o_a, 2 * h_dn), seg(o_b, 2 * h_dn),
         jnp.zeros((depth, d, LANES - n_small + n_pad), F32)], axis=-1).astype(BF16)
    conv_w = jnp.concatenate([dn_conv_w, ssd_conv_w], axis=-1)
    conv_b = jnp.concatenate([jnp.zeros((depth, 3 * w), F32), ssd_conv_b], axis=-1)[:, None, :]
    conv_width = 3 * w + xbc_w
    small_bias = jnp.concatenate([ssd_dt_bias.reshape(depth, -1), dn_dt_bias.reshape(depth, -1),
                                  jnp.zeros((depth, LANES - 2 * h_ssd - 2 * h_dn), F32)], axis=-1)[:, None, :]

    ssd_aneg = jnp.repeat(-jnp.exp(ssd_a_log), CHUNK, axis=-1)[:, :, None, :]
    ssd_d_exp = jnp.repeat(ssd_d, CHUNK, axis=-1)[:, None, :]
    dn_nega = jnp.concatenate([jnp.zeros((depth, 2 * h_ssd), F32), -jnp.exp(dn_a_log).reshape(depth, -1),
                               jnp.zeros((depth, LANES - 2 * h_ssd - 2 * h_dn), F32)], axis=-1)[:, None, :]
    dn_norm_t = jnp.tile(dn_norm, (1, h_dn))[:, None, :]
    e_dt = _expand_matrix(lambda dd, h: dd * h_ssd + h, h_ssd, CHUNK)
    e_g = _expand_matrix(lambda dd, h: 2 * h_ssd + dd * h_dn + h, h_dn, LANES)
    e_gc = _expand_matrix(lambda dd, h: 2 * h_ssd + dd * h_dn + h, h_dn, CHUNK)
    e_beta = _expand_matrix(lambda dd, h: 2 * h_ssd + 2 * h_dn + dd * h_dn + h, h_dn, LANES)
    ssd_consts = _chunk_consts(w)
    dn_consts = _chunk_consts(h_dn * CHUNK)

    gpc = LANES // js
    abar_re, abar_im, bb_re, bb_im = _s5_discretise(s5_lam_re, s5_lam_im, s5_log_step, s5_b_re, s5_b_im)
    n_s5 = gs * ps
    abar_re = jnp.broadcast_to(abar_re.reshape(depth, 2, 1, n_s5), (depth, 2, SUBLANES, n_s5))
    abar_im = jnp.broadcast_to(abar_im.reshape(depth, 2, 1, n_s5), (depth, 2, SUBLANES, n_s5))
    bb_re = _block_diag_chunks(jnp.swapaxes(bb_re, -1, -2), gpc).astype(BF16)
    bb_im = _block_diag_chunks(jnp.swapaxes(bb_im, -1, -2), gpc).astype(BF16)
    cc_re = _block_diag_chunks(jnp.swapaxes(s5_c_re, -1, -2), gpc).astype(BF16)
    cc_im = _block_diag_chunks(jnp.swapaxes(s5_c_im, -1, -2), gpc).astype(BF16)

    ssd_state_t = jnp.swapaxes(state_ssd.reshape(n_smp, depth, 2, w, n_state), -1, -2)
    s5_state = state_s5.reshape(n_smp // SUBLANES, SUBLANES, depth, 2, 2, n_s5).transpose(0, 2, 3, 4, 1, 5)

    w_glu_b = s5_w_glu.astype(BF16)
    w_branch_b = w_branch.astype(BF16)
    w_out_b = w_out.astype(BF16)
    b_up4 = b_up[:, :, None, :]
    b_down4 = b_down[:, :, None, :]
    norm1_3 = norm1[:, None, :]
    norm2_3 = norm2[:, None, :]
    b_router3 = b_router[:, None, :]
    ssd_norm3 = ssd_norm[:, None, :]
    s5_d3 = s5_d[:, None, :]

    cond = jnp.zeros((2 * SUBLANES, d), F32).at[0].set(c_ctx).at[1:1 + n_smp].set(c)
    mod = _ada_call(cond, w_ada, b_ada).reshape(depth, 2 * SUBLANES, 6, d)

    xs = x_sample + _grid_pos(l_smp, d)[None]
    x = jnp.concatenate([x_prompt.reshape(lay.t_ctx, d), xs.reshape(n_smp * l_smp, d)], axis=0)

    def to_time_major(a):
        cw = a.shape[-1]
        ctx = a[:lay.t_ctx].reshape(n_ctx // SUBLANES, SUBLANES, l_ctx, cw).transpose(0, 2, 1, 3)
        smp = a[lay.t_ctx:].reshape(n_smp // SUBLANES, SUBLANES, l_smp, cw).transpose(0, 2, 1, 3)
        return jnp.concatenate([ctx.reshape(lay.t_ctx, cw), smp.reshape(t - lay.t_ctx, cw)], axis=0)

    def from_time_major(a):
        cw = a.shape[-1]
        lead = a.shape[:-2]
        ctx = a[..., :lay.t_ctx, :].reshape(lead + (n_ctx // SUBLANES, l_ctx, SUBLANES, cw))
        smp = a[..., lay.t_ctx:, :].reshape(lead + (n_smp // SUBLANES, l_smp, SUBLANES, cw))
        ctx = jnp.swapaxes(ctx, -2, -3).reshape(lead + (lay.t_ctx, cw))
        smp = jnp.swapaxes(smp, -2, -3).reshape(lead + (t - lay.t_ctx, cw))
        return jnp.concatenate([ctx, smp], axis=-2)

    bm = 256
    ssd_out, s5_out, dn_out = [], [], []
    for l in range(depth):
        proj = _in_proj_call(lay, l, x, norm1_3, mod, w_in_r)
        conv = _conv_call(lay, l, proj, conv_w, conv_b, col_qkv, conv_width, 2 * w)
        y_ssd, ssd_fin = _ssd_call(lay, l, conv, proj, col_small, 3 * w, 4 * w, ssd_consts, e_dt, small_bias,
                                   ssd_aneg, ssd_state_t)
        u_tm = to_time_major(proj[:, col_u:col_u + w])
        y_s5_tm, s5_fin = _s5_call(lay, l, u_tm, bb_re, bb_im, cc_re, cc_im, abar_re, abar_im, s5_state)
        y_s5 = from_time_major(y_s5_tm)
        o_dn, dn_fin = _dn_call(lay, l, conv, proj, col_small, dn_consts, e_beta, e_g, e_gc, small_bias,
                                dn_nega, state_delta)
        ys = _branch_call(lay, l, y_ssd, conv, 3 * w, proj, col_z, col_u, col_g, y_s5, o_dn,
                          ssd_d_exp, ssd_norm3, s5_d3, dn_norm_t, w_glu_b)
        merged = _merge_call(lay, l, ys, proj, col_gate, w_branch_b)
        x = _out_proj_call(lay, l, merged, w_out_b, x, mod)
        x = _moe(lay, l, x, norm2_3, mod, w_router, b_router3, w_up, b_up4, w_down, b_down4, bm)

        ssd_out.append(jnp.swapaxes(ssd_fin[:n_ctx], -1, -2).reshape(n_ctx, 2, h_ssd, CHUNK, n_state))
        ng = n_ctx // SUBLANES
        s5_out.append(s5_fin[:ng].transpose(0, 3, 1, 2, 4).reshape(n_ctx, 2, 2, gs, ps))
        dn_out.append(dn_fin[:n_ctx])

    y = _final_norm_call(x, final_norm)
    y_prompt = y[:lay.t_ctx].reshape(n_ctx, l_ctx, d)
    y_sample = y[lay.t_ctx:].reshape(n_smp, l_smp, d)
    return (y_prompt, y_sample, jnp.stack(ssd_out, axis=1), jnp.stack(s5_out, axis=1), jnp.stack(dn_out, axis=1))
```
